```python
import math
import jax, jax.numpy as jnp
from jax import lax
import numpy as np

D_MODEL = 2048
BATCH = 8
SEQ = 4096
DEPTH = 4

GRID_W = 64
CTX_LEN = 256
EPS = 1e-6
ROPE_BASE = 10000.0
NEG_BIG = -1e30

N_BRANCH = 3
BRANCH_W = D_MODEL // 2

A_HEADS = 8
A_DK = 128
A_DV = BRANCH_W // A_HEADS
A_KW = A_HEADS * A_DK
A_CHUNK = 64

B_HEADS = 8
B_DH = BRANCH_W // (2 * B_HEADS)
B_QBLOCK = 128

C_Q_HEADS = 8
C_KV_HEADS = 2
C_GROUP = C_Q_HEADS // C_KV_HEADS
C_DH = BRANCH_W // C_Q_HEADS
C_WINDOW = 128
C_BLOCK = 128

D_FF = 5632

IN_SIZES = (A_KW, A_KW, A_KW, BRANCH_W, BRANCH_W,
            2 * B_HEADS * B_DH, 2 * B_HEADS * B_DH, 2 * B_HEADS * B_DH,
            C_Q_HEADS * C_DH, C_KV_HEADS * C_DH, C_KV_HEADS * C_DH,
            N_BRANCH * D_MODEL)
N_IN = sum(IN_SIZES)

kernel_name = 'hybrid_gated_merge_dit_block'


def rms_norm(x, g):
    xf = x.astype(jnp.float32)
    xf = xf * lax.rsqrt(jnp.mean(jnp.square(xf), axis=-1, keepdims=True) + EPS)
    return (xf * g.astype(jnp.float32)).astype(x.dtype)


def modulate(h, shift, scale):
    return h * (1 + scale) + shift


def axial_angles(n_tok, head_dim):
    rows = n_tok // GRID_W
    pos = jnp.arange(rows * GRID_W, dtype=jnp.int32)
    row = (pos // GRID_W).astype(jnp.float32)
    col = (pos % GRID_W).astype(jnp.float32)
    quarter = head_dim // 4
    inv = jnp.power(ROPE_BASE, -jnp.arange(quarter, dtype=jnp.float32) / quarter)
    return row[:, None] * inv, col[:, None] * inv


def rope_1d(x, ang):
    x1, x2 = jnp.split(x, 2, axis=-1)
    cos, sin = jnp.cos(ang).astype(x.dtype), jnp.sin(ang).astype(x.dtype)
    return jnp.concatenate([x1 * cos - x2 * sin, x2 * cos + x1 * sin], axis=-1)


def axial_rope(x, angles):
    row_ang, col_ang = angles
    shape = (x.shape[1],) + (1,) * (x.ndim - 3) + (x.shape[-1] // 4,)
    xr, xc = jnp.split(x, 2, axis=-1)
    return jnp.concatenate([rope_1d(xr, row_ang.reshape(shape)), rope_1d(xc, col_ang.reshape(shape))], axis=-1)


def sink_softmax(logits, sink):
    full = jnp.concatenate([logits, jnp.broadcast_to(sink, logits.shape[:-1] + (1,))], axis=-1)
    return jax.nn.softmax(full, axis=-1)[..., :-1]


def hgrn_lower_bounds(logits):
    p = jax.nn.softmax(logits.astype(jnp.float32), axis=1)
    return jnp.cumsum(p, axis=1) - p[:, :1]


def hgrn2_chunk_scan(q, k, v, log_f, s0):
    bsz, n_tok = q.shape[:2]
    n_chunk = n_tok // A_CHUNK

    def chunks(t):
        return jnp.moveaxis(t.reshape(bsz, n_chunk, A_CHUNK, A_HEADS, t.shape[-1]), 1, 0)

    incl = jnp.tril(jnp.ones((A_CHUNK, A_CHUNK), dtype=bool))[None, :, :, None, None]

    def step(state, inp):
        qc, kc, vc, gc = inp
        b = jnp.cumsum(gc, axis=1)
        rel = jnp.where(incl, b[:, :, None] - b[:, None, :], 0.0)
        decay = jnp.where(incl, jnp.exp(rel), 0.0)
        scores = jnp.einsum('bthd,bshd,btshd->bhts', qc, kc, decay)
        o = (jnp.einsum('bhts,bshe->bthe', scores, vc)
             + jnp.einsum('bthd,bhde->bthe', qc * jnp.exp(b), state))
        b_end = b[:, -1]
        state = (jnp.exp(b_end)[..., None] * state
                 + jnp.einsum('bshd,bshe->bhde', kc * jnp.exp(b_end[:, None] - b), vc))
        return state, o

    s_fin, o = lax.scan(step, s0, (chunks(q), chunks(k), chunks(v), chunks(log_f)))
    return jnp.moveaxis(o, 0, 1).reshape(bsz, n_tok, A_HEADS, A_DV), s_fin


def hgrn2_branch(lat_parts, ctx_parts, lb_fwd, lb_bwd, norm_g, need_ctx):
    flip = lambda t: jnp.flip(t, axis=1)

    def forget(z, lb):
        zf = z.astype(jnp.float32).reshape(z.shape[0], z.shape[1], A_HEADS, A_DK)
        lbh = lb.reshape(A_HEADS, A_DK)
        f = lbh + (1 - lbh) * jax.nn.sigmoid(zf)
        k = (1 - lbh) * jax.nn.sigmoid(-zf)
        return k, jnp.log(f)

    def bidir(parts, s_fwd0, s_bwd0):
        q, f_fwd, f_bwd, i, _ = parts
        bsz, n_tok = q.shape[:2]
        qh = jax.nn.silu(q.astype(jnp.float32)).reshape(bsz, n_tok, A_HEADS, A_DK)
        vh = i.astype(jnp.float32).reshape(bsz, n_tok, A_HEADS, A_DV)
        k_f, g_f = forget(f_fwd, lb_fwd)
        k_b, g_b = forget(f_bwd, lb_bwd)
        o_f, s_f = hgrn2_chunk_scan(qh, k_f, vh, g_f, s_fwd0)
        o_b, s_b = hgrn2_chunk_scan(flip(qh), flip(k_b), flip(vh), flip(g_b), s_bwd0)
        return o_f + flip(o_b), s_f, s_b

    def finish(o, g):
        bsz, n_tok = o.shape[:2]
        o = rms_norm(o, norm_g).reshape(bsz, n_tok, BRANCH_W)
        return (o * jax.nn.silu(g.astype(jnp.float32))).astype(g.dtype)

    bsz = lat_parts[0].shape[0]
    zero = jnp.zeros((bsz, A_HEADS, A_DK, A_DV), jnp.float32)
    o_ctx, s_f, s_b = bidir(ctx_parts, zero, zero)
    o_lat, _, _ = bidir(lat_parts, s_f, s_b)
    out_ctx = finish(o_ctx, ctx_parts[4]) if need_ctx else None
    return finish(o_lat, lat_parts[4]), out_ctx


def diff_attention(q, k, v, lam):
    s = jnp.einsum('bqhcd,bkhcd->bhcqk', q, k).astype(jnp.float32) * (B_DH ** -0.5)
    p = jax.nn.softmax(s, axis=-1)
    w = (p[:, :, 0] - lam * p[:, :, 1]).astype(v.dtype)
    return jnp.einsum('bhqk,bkhe->bqhe', w, v)


def diff_attn_branch(lat_parts, ctx_parts, lam_vecs, subln_g, lambda_init, rope, need_ctx):
    def heads(q, k, v):
        bsz, n_tok = q.shape[:2]
        return (q.reshape(bsz, n_tok, B_HEADS, 2, B_DH),
                k.reshape(bsz, n_tok, B_HEADS, 2, B_DH),
                v.reshape(bsz, n_tok, B_HEADS, 2 * B_DH))

    ql, kl, vl = heads(*lat_parts)
    qc, kc, vc = heads(*ctx_parts)
    ql, kl = axial_rope(ql, rope), axial_rope(kl, rope)
    lv = lam_vecs.astype(jnp.float32)
    lam = jnp.exp(jnp.sum(lv[0] * lv[1])) - jnp.exp(jnp.sum(lv[2] * lv[3])) + lambda_init

    def finish(o):
        bsz, n_tok = o.shape[:2]
        return (rms_norm(o, subln_g) * (1 - lambda_init)).reshape(bsz, n_tok, BRANCH_W)

    k_all = jnp.concatenate([kc, kl], axis=1)
    v_all = jnp.concatenate([vc, vl], axis=1)
    bsz, n_tok = ql.shape[:2]
    n_blk = n_tok // B_QBLOCK
    qb = jnp.moveaxis(ql.reshape(bsz, n_blk, B_QBLOCK, B_HEADS, 2, B_DH), 1, 0)
    ob = lax.map(lambda q_blk: diff_attention(q_blk, k_all, v_all, lam), qb)
    o_lat = jnp.moveaxis(ob, 0, 1).reshape(bsz, n_tok, B_HEADS, 2 * B_DH)
    out_ctx = finish(diff_attention(qc, kc, vc, lam)) if need_ctx else None
    return finish(o_lat), out_ctx


def window_attn_branch(lat_parts, ctx_parts, sink, rope, need_ctx):
    def heads(q, k, v):
        bsz, n_tok = q.shape[:2]
        return (q.reshape(bsz, n_tok, C_KV_HEADS, C_GROUP, C_DH),
                k.reshape(bsz, n_tok, C_KV_HEADS, C_DH),
                v.reshape(bsz, n_tok, C_KV_HEADS, C_DH))

    ql, kl, vl = heads(*lat_parts)
    qc, kc, vc = heads(*ctx_parts)
    ql, kl = axial_rope(ql, rope), axial_rope(kl, rope)
    sink_gr = sink.astype(jnp.float32).reshape(C_KV_HEADS, C_GROUP)
    scale = C_DH ** -0.5
    bsz, n_tok = ql.shape[:2]
    n_ctx = kc.shape[1]
    n_blk = n_tok // C_BLOCK
    qb = ql.reshape(bsz, n_blk, C_BLOCK, C_KV_HEADS, C_GROUP, C_DH)

    def band(t):
        tb = jnp.pad(t.reshape(bsz, n_blk, C_BLOCK, C_KV_HEADS, C_DH), ((0, 0), (1, 1), (0, 0), (0, 0), (0, 0)))
        return jnp.concatenate([tb[:, :-2], tb[:, 1:-1], tb[:, 2:]], axis=2)

    kb, vb = band(kl), band(vl)
    qi = jnp.arange(C_BLOCK)[:, None]
    kj = jnp.arange(3 * C_BLOCK)[None, :]
    in_window = jnp.abs(kj - C_BLOCK - qi) <= C_WINDOW
    k_pos = (jnp.arange(n_blk)[:, None] - 1) * C_BLOCK + kj
    in_range = (k_pos >= 0) & (k_pos < n_tok)
    mask = in_window[None] & in_range[:, None, :]
    s_loc = jnp.einsum('bnqgrd,bnkgd->bngrqk', qb, kb).astype(jnp.float32) * scale
    s_loc = jnp.where(mask[None, :, None, None], s_loc, NEG_BIG)
    s_ctx = jnp.einsum('bnqgrd,blgd->bngrql', qb, kc).astype(jnp.float32) * scale
    p = sink_softmax(jnp.concatenate([s_ctx, s_loc], axis=-1),
                     sink_gr[None, None, :, :, None, None]).astype(vl.dtype)
    o = (jnp.einsum('bngrql,blgd->bnqgrd', p[..., :n_ctx], vc)
         + jnp.einsum('bngrqk,bnkgd->bnqgrd', p[..., n_ctx:], vb))
    out_lat = o.reshape(bsz, n_tok, BRANCH_W)
    out_ctx = None
    if need_ctx:
        s = jnp.einsum('blgrd,bmgd->bgrlm', qc, kc).astype(jnp.float32) * scale
        pc = sink_softmax(s, sink_gr[None, :, :, None, None]).astype(vc.dtype)
        out_ctx = jnp.einsum('bgrlm,bmgd->blgrd', pc, vc).reshape(bsz, n_ctx, BRANCH_W)
    return out_lat, out_ctx


def merge_branches(o_a, o_b, o_c, z_gate, w_branch_l, w_out_l):
    bsz, n_tok = o_a.shape[:2]
    br = jnp.stack([o_a, o_b, o_c], axis=2)
    proj = jnp.einsum('btnw,nwd->btnd', br, w_branch_l)
    gates = jax.nn.sigmoid(z_gate.reshape(bsz, n_tok, N_BRANCH, D_MODEL))
    return jnp.einsum('btnd,btnd->btd', gates, proj) @ w_out_l


def mixing_sublayer(h_lat, h_ctx, w_in_l, lb_fwd, lb_bwd, a_norm_g_l, b_lam_l, b_subln_g_l,
                    lambda_init, c_sink_l, w_branch_l, w_out_l, rope_b, rope_c, need_ctx):
    cuts = np.cumsum(IN_SIZES)[:-1].tolist()
    zl = jnp.split(h_lat @ w_in_l, cuts, axis=-1)
    zc = jnp.split(h_ctx @ w_in_l, cuts, axis=-1)
    a_lat, a_ctx = hgrn2_branch(zl[0:5], zc[0:5], lb_fwd, lb_bwd, a_norm_g_l, need_ctx)
    b_lat, b_ctx = diff_attn_branch(zl[5:8], zc[5:8], b_lam_l, b_subln_g_l, lambda_init, rope_b, need_ctx)
    c_lat, c_ctx = window_attn_branch(zl[8:11], zc[8:11], c_sink_l, rope_c, need_ctx)
    out_lat = merge_branches(a_lat, b_lat, c_lat, zl[11], w_branch_l, w_out_l)
    out_ctx = merge_branches(a_ctx, b_ctx, c_ctx, zc[11], w_branch_l, w_out_l) if need_ctx else None
    return out_lat, out_ctx


def dwconv3(u, w, b):
    up = jnp.pad(u, ((0, 0), (1, 1), (0, 0)))
    return up[:, :-2] * w[0] + up[:, 1:-1] * w[1] + up[:, 2:] * w[2] + b


def conv_ffn(h, w_up_l, conv_w_l, conv_b_l, w_down_l):
    u = dwconv3(h @ w_up_l, conv_w_l, conv_b_l)
    a, v = jnp.split(u, 2, axis=-1)
    return (jax.nn.silu(a) * v) @ w_down_l


def setup_inputs(seed: int = 0) -> dict:
    key = jax.random.key(seed)
    ks = jax.random.split(key, 22)

    def nrm(k, shape, scale):
        return jax.random.normal(k, shape, jnp.float32) * scale

    def gain(k, shape):
        return 1.0 + nrm(k, shape, 0.05)

    return {
        'x': nrm(ks[0], (BATCH, SEQ, D_MODEL), 1.0),
        'c': nrm(ks[1], (BATCH, D_MODEL), 1.0),
        'ctx': nrm(ks[2], (BATCH, CTX_LEN, D_MODEL), 1.0),
        'c_ctx': nrm(ks[3], (D_MODEL,), 1.0),
        'w_ada': nrm(ks[4], (DEPTH, D_MODEL, 6 * D_MODEL), 0.5 * D_MODEL ** -0.5),
        'b_ada': nrm(ks[5], (DEPTH, 6 * D_MODEL), 0.02),
        'g_pre_mix': gain(ks[6], (DEPTH, D_MODEL)),
        'g_post_mix': gain(ks[7], (DEPTH, D_MODEL)),
        'g_pre_ffn': gain(ks[8], (DEPTH, D_MODEL)),
        'g_post_ffn': gain(ks[9], (DEPTH, D_MODEL)),
        'w_in': nrm(ks[10], (DEPTH, D_MODEL, N_IN), D_MODEL ** -0.5),
        'a_lb_logits': nrm(ks[11], (2, DEPTH, A_KW), 0.5),
        'a_norm_g': gain(ks[12], (DEPTH, A_DV)),
        'b_lambda': nrm(ks[13], (DEPTH, 4, B_DH), 0.1),
        'b_subln_g': gain(ks[14], (DEPTH, 2 * B_DH)),
        'c_sink': nrm(ks[15], (DEPTH, C_Q_HEADS), 0.5),
        'w_branch': nrm(ks[16], (DEPTH, N_BRANCH, BRANCH_W, D_MODEL), BRANCH_W ** -0.5),
        'w_out': nrm(ks[17], (DEPTH, D_MODEL, D_MODEL), D_MODEL ** -0.5),
        'w_up': nrm(ks[18], (DEPTH, D_MODEL, 2 * D_FF), D_MODEL ** -0.5),
        'conv_w': nrm(ks[19], (DEPTH, 3, 2 * D_FF), 3 ** -0.5),
        'conv_b': nrm(ks[20], (DEPTH, 2 * D_FF), 0.02),
        'w_down': nrm(ks[21], (DEPTH, D_FF, D_MODEL), D_FF ** -0.5),
    }


def reference(x, c, ctx, c_ctx, w_ada, b_ada, g_pre_mix, g_post_mix, g_pre_ffn, g_post_ffn,
              w_in, a_lb_logits, a_norm_g, b_lambda, b_subln_g, c_sink, w_branch, w_out,
              w_up, conv_w, conv_b, w_down):
    n_tok = x.shape[1]
    rope_b = axial_angles(n_tok, B_DH)
    rope_c = axial_angles(n_tok, C_DH)
    lower = hgrn_lower_bounds(a_lb_logits)
    silu_c = jax.nn.silu(c)
    silu_cc = jax.nn.silu(c_ctx)
    for l in range(DEPTH):
        need_ctx = l < DEPTH - 1
        mod_lat = [m[:, None, :] for m in jnp.split(silu_c @ w_ada[l] + b_ada[l], 6, axis=-1)]
        mod_ctx = jnp.split(silu_cc @ w_ada[l] + b_ada[l], 6, axis=-1)
        lambda_init = 0.8 - 0.6 * math.exp(-0.3 * l)
        h_lat = modulate(rms_norm(x, g_pre_mix[l]), mod_lat[0], mod_lat[1])
        h_ctx = modulate(rms_norm(ctx, g_pre_mix[l]), mod_ctx[0], mod_ctx[1])
        m_lat, m_ctx = mixing_sublayer(h_lat, h_ctx, w_in[l], lower[0, l], lower[1, l], a_norm_g[l],
                                       b_lambda[l], b_subln_g[l], lambda_init, c_sink[l],
                                       w_branch[l], w_out[l], rope_b, rope_c, need_ctx)
        x = x + mod_lat[2] * rms_norm(m_lat, g_post_mix[l])
        f_lat = conv_ffn(modulate(rms_norm(x, g_pre_ffn[l]), mod_lat[3], mod_lat[4]),
                         w_up[l], conv_w[l], conv_b[l], w_down[l])
        x = x + mod_lat[5] * rms_norm(f_lat, g_post_ffn[l])
        if need_ctx:
            ctx = ctx + mod_ctx[2] * rms_norm(m_ctx, g_post_mix[l])
            f_ctx = conv_ffn(modulate(rms_norm(ctx, g_pre_ffn[l]), mod_ctx[3], mod_ctx[4]),
                             w_up[l], conv_w[l], conv_b[l], w_down[l])
            ctx = ctx + mod_ctx[5] * rms_norm(f_ctx, g_post_ffn[l])
    return x
```

```python
import functools
import math

import numpy as np
import jax
import jax.numpy as jnp
from jax import lax
from jax.experimental import pallas as pl
from jax.experimental.pallas import tpu as pltpu

F32 = jnp.float32
BF16 = jnp.bfloat16

D_MODEL = 2048
GRID_W = 64
EPS = 1e-6
ROPE_BASE = 10000.0
NEG_BIG = -1e30
BRANCH_W = D_MODEL // 2
A_HEADS = 8
A_DK = 128
A_DV = BRANCH_W // A_HEADS
B_HEADS = 8
B_DH = BRANCH_W // (2 * B_HEADS)
C_Q_HEADS = 8
C_KV_HEADS = 2
C_GROUP = C_Q_HEADS // C_KV_HEADS
C_DH = BRANCH_W // C_Q_HEADS
C_BLOCK = 128
D_FF = 5632
N_BRANCH = 3
LOG2E = 1.4426950408889634

LANES = 128
VMEM_LIMIT_BYTES = 56 * 1024 * 1024
MOD_ROWS = 16
HGRN_CHUNK = 64
HGRN_HEADS_PER_STEP = 2
ONES_ROWS = 16


def _cparams(*sem):
    return pltpu.CompilerParams(dimension_semantics=sem, vmem_limit_bytes=VMEM_LIMIT_BYTES)


def _pick(cands, *dims):
    for c in cands:
        if all(d % c == 0 for d in dims):
            return c
    raise ValueError(f"no tile in {cands} divides {dims}")


def _dot(a, b):
    return jnp.dot(a, b, preferred_element_type=F32)


def _dot_nt(a, b):
    return lax.dot_general(a, b, (((1,), (1,)), ((), ())), preferred_element_type=F32)


def _dot_tn(a, b):
    return lax.dot_general(a, b, (((0,), (0,)), ((), ())), preferred_element_type=F32)


def _rms(x):
    return lax.rsqrt(jnp.mean(x * x, axis=-1, keepdims=True) + EPS)


def _silu(x):
    return x * jax.nn.sigmoid(x)


def _mod_kernel(c_ref, w_ref, b_ref, o_ref):
    s = _silu(c_ref[...])
    o_ref[0] = _dot(s.astype(BF16), w_ref[0].astype(BF16)) + b_ref[0]


def _modulation(cc, w_ada, b_ada):
    depth, d, n = w_ada.shape
    tn = _pick((1024, 512, 256, 128), n)
    return pl.pallas_call(
        _mod_kernel,
        grid=(depth, n // tn),
        in_specs=[pl.BlockSpec((MOD_ROWS, d), lambda l, j: (0, 0)),
                  pl.BlockSpec((1, d, tn), lambda l, j: (l, 0, j)),
                  pl.BlockSpec((1, 1, tn), lambda l, j: (l, 0, j))],
        out_specs=pl.BlockSpec((1, MOD_ROWS, tn), lambda l, j: (l, 0, j)),
        out_shape=jax.ShapeDtypeStruct((depth, MOD_ROWS, n), F32),
        compiler_params=_cparams("parallel", "parallel"),
        name="modulation",
    )(cc, w_ada, b_ada.reshape(depth, 1, n))


def _prenorm_kernel(x_ref, g_ref, sh_ref, sc_ref, h_ref):
    x = x_ref[...]
    h = (x * _rms(x) * g_ref[...]) * (1.0 + sc_ref[0]) + sh_ref[0]
    h_ref[...] = h.astype(h_ref.dtype)


def _prenorm(x, g, sh, sc, geo):
    t, d = x.shape
    tm = geo["tm_row"]
    midx = geo["midx"](tm)
    return pl.pallas_call(
        _prenorm_kernel,
        grid=(t // tm,),
        in_specs=[pl.BlockSpec((tm, d), lambda i: (i, 0)),
                  pl.BlockSpec((1, d), lambda i: (0, 0)),
                  pl.BlockSpec((1, 1, d), lambda i: (midx(i), 0, 0)),
                  pl.BlockSpec((1, 1, d), lambda i: (midx(i), 0, 0))],
        out_specs=pl.BlockSpec((tm, d), lambda i: (i, 0)),
        out_shape=jax.ShapeDtypeStruct((t, d), BF16),
        compiler_params=_cparams("parallel"),
        name="prenorm",
    )(x, g.reshape(1, d), sh, sc)


def _proj_kernel(h_ref, w_ref, *rest, epi, group):
    acc = _dot(h_ref[...], w_ref[...])
    if epi == "none":
        (o_ref,) = rest
        o_ref[...] = acc.astype(o_ref.dtype)
    elif epi == "sigmoid":
        (o_ref,) = rest
        o_ref[...] = jax.nn.sigmoid(acc).astype(o_ref.dtype)
    else:
        cos_ref, sin_ref, cs_ref, o_ref = rest
        tn = acc.shape[1]
        reps = tn // LANES
        cos = jnp.concatenate([cos_ref[...]] * reps, axis=1)
        sin = jnp.concatenate([sin_ref[...]] * reps, axis=1)
        lane = lax.broadcasted_iota(jnp.int32, acc.shape, 1)
        first = (lane % (2 * group)) < group
        partner = jnp.where(first, pltpu.roll(acc, tn - group, 1), pltpu.roll(acc, group, 1))
        o_ref[...] = ((acc * cos + partner * sin) * cs_ref[...]).astype(o_ref.dtype)


def _project(h, w, out_dtype, geo, epi="none", rope=None, tn_cands=(512, 256, 128)):
    t, d = h.shape
    n = w.shape[1]
    tm = geo["tm_mm"]
    tn = _pick(tn_cands, n)
    in_specs = [pl.BlockSpec((tm, d), lambda i, j: (i, 0)),
                pl.BlockSpec((d, tn), lambda i, j: (0, j))]
    args = [h, w]
    group = 0
    if epi == "rope":
        cos, sin, colscale, group = rope
        tab = geo["tabidx"](tm)
        in_specs += [pl.BlockSpec((tm, LANES), lambda i, j: (tab(i), 0)),
                     pl.BlockSpec((tm, LANES), lambda i, j: (tab(i), 0)),
                     pl.BlockSpec((1, tn), lambda i, j: (0, j))]
        args += [cos, sin, colscale]
    return pl.pallas_call(
        functools.partial(_proj_kernel, epi=epi, group=group),
        grid=(t // tm, n // tn),
        in_specs=in_specs,
        out_specs=pl.BlockSpec((tm, tn), lambda i, j: (i, j)),
        out_shape=jax.ShapeDtypeStruct((t, n), out_dtype),
        compiler_params=_cparams("parallel", "arbitrary"),
        name=f"proj_{epi}",
    )(*args)


def _hgrn_constants():
    c = HGRN_CHUNK
    nlev = int(math.log2(c))
    t = np.arange(c)[:, None]
    u = np.arange(c)[None, :]
    w = np.zeros((2, (2 + nlev) * c + ONES_ROWS, c), np.float32)
    lev = np.full((2, c, c), -1, np.int32)
    for d in range(2):
        if d == 0:
            cum, rem = u <= t, u > t
        else:
            cum, rem = u >= t, u < t
        w[d, 0:c] = cum
        w[d, c:2 * c] = rem
        for l in range(nlev):
            h = 1 << l
            base = (t // (2 * h)) * (2 * h)
            if d == 0:
                m = base + h - 1
                e = np.where(t > m, (u > m) & (u <= t), (u > t) & (u <= m))
                pair = (t // (2 * h) == u // (2 * h)) & (t % (2 * h) >= h) & (u % (2 * h) < h)
            else:
                m = base + h
                e = np.where(t < m, (u >= t) & (u < m), (u >= m) & (u < t))
                pair = (t // (2 * h) == u // (2 * h)) & (t % (2 * h) < h) & (u % (2 * h) >= h)
            w[d, (2 + l) * c:(3 + l) * c] = e
            lev[d][pair] = l
        w[d, (2 + nlev) * c:] = 1.0
        lev[d][np.arange(c), np.arange(c)] = nlev
    return jnp.asarray(w, BF16), jnp.asarray(lev), nlev


def _hgrn_kernel(zq_ref, zf_ref, zi_ref, zg_ref, lb_ref, ng_ref, w_ref, lev_ref, o_ref,
                 st_ref, ofwd_ref, *, nlev, n_ctx_blk, n_lat_blk):
    c = HGRN_CHUNK
    rows = zq_ref.shape[0]
    n_chunks = rows // c
    d = pl.program_id(2)
    j = pl.program_id(3)

    @pl.when(j == 0)
    def _():
        st_ref[...] = jnp.zeros_like(st_ref)

    slot = jnp.where(d == 0, j,
                     jnp.where(j < n_ctx_blk, n_ctx_blk - 1 - j,
                               2 * n_ctx_blk + n_lat_blk - 1 - j))
    lb = lb_ref[0]
    w = w_ref[0]
    lev = lev_ref[0]
    ng = ng_ref[...]

    def chunk(i, carry):
        idx = jnp.where(d == 0, i, n_chunks - 1 - i)
        off = pl.multiple_of(idx * c, c)
        rs = pl.ds(off, c)
        zf = zf_ref[rs, :]
        sig = jax.nn.sigmoid(zf)
        f = lb + (1.0 - lb) * sig
        kk = (1.0 - lb) * (1.0 - sig)
        g = jnp.log(f)
        expo = _dot(w, g.astype(BF16))
        a = jnp.exp(expo)
        q = _silu(zq_ref[rs, :].astype(F32))
        v = zi_ref[rs, :]
        outs = []
        for hh in range(HGRN_HEADS_PER_STEP):
            ls = slice(hh * LANES, (hh + 1) * LANES)
            qh, kh, vh = q[:, ls], kk[:, ls], v[:, ls]
            scores = jnp.zeros((c, c), F32)
            for l in range(nlev):
                al = a[(2 + l) * c:(3 + l) * c, ls]
                p = _dot_nt((qh * al).astype(BF16), (kh * al).astype(BF16))
                scores = jnp.where(lev == l, p, scores)
            p = _dot_nt(qh.astype(BF16), kh.astype(BF16))
            scores = jnp.where(lev == nlev, p, scores)
            st = st_ref[hh]
            o = _dot(scores.astype(BF16), vh) + _dot_nt((qh * a[0:c, ls]).astype(BF16), st.astype(BF16))
            ut = _dot_tn(vh, (kh * a[c:2 * c, ls]).astype(BF16))
            dec = a[(2 + nlev) * c:(2 + nlev) * c + 1, ls]
            st_ref[hh] = st * dec + ut
            outs.append(o)
        o2 = jnp.concatenate(outs, axis=1)
        srow = pl.ds(pl.multiple_of(slot * rows + off, c), c)

        @pl.when(d == 0)
        def _():
            ofwd_ref[srow, :] = o2

        @pl.when(d == 1)
        def _():
            tot = ofwd_ref[srow, :] + o2
            gate = _silu(zg_ref[rs, :].astype(F32))
            fin = []
            for hh in range(HGRN_HEADS_PER_STEP):
                ls = slice(hh * LANES, (hh + 1) * LANES)
                th = tot[:, ls]
                fin.append(th * _rms(th) * ng[:, ls])
            o_ref[rs, :] = (jnp.concatenate(fin, axis=1) * gate).astype(o_ref.dtype)

        return carry

    lax.fori_loop(0, n_chunks, chunk, 0)


def _hgrn(z_nr, z_f, lb, norm_g, consts, geo):
    w_c, lev_c, nlev = consts
    bsz, s, l = geo["B"], geo["S"], geo["L"]
    t = z_nr.shape[0]
    r = geo["hgrn_rows"]
    nc, nl = l // r, s // r
    wcols = HGRN_HEADS_PER_STEP * LANES
    hp = BRANCH_W // wcols
    ctx0 = bsz * s // r

    def rowblk(b, d, j):
        slot = jnp.where(d == 0, j, jnp.where(j < nc, nc - 1 - j, 2 * nc + nl - 1 - j))
        return jnp.where(slot < nc, ctx0 + b * nc + slot, b * nl + slot - nc)

    def zmap(colblk0):
        return lambda b, p, d, j: (rowblk(b, d, j), colblk0 + p)

    def outmap(b, p, d, j):
        return (rowblk(b, 1, jnp.where(d == 0, 0, j)), p)

    kern = functools.partial(_hgrn_kernel, nlev=nlev, n_ctx_blk=nc, n_lat_blk=nl)
    return pl.pallas_call(
        kern,
        grid=(bsz, hp, 2, nc + nl),
        in_specs=[pl.BlockSpec((r, wcols), zmap(0)),
                  pl.BlockSpec((r, wcols), lambda b, p, d, j: (rowblk(b, d, j), d * hp + p)),
                  pl.BlockSpec((r, wcols), zmap(hp)),
                  pl.BlockSpec((r, wcols), zmap(2 * hp)),
                  pl.BlockSpec((1, 1, wcols), lambda b, p, d, j: (d, 0, p)),
                  pl.BlockSpec((1, wcols), lambda b, p, d, j: (0, p)),
                  pl.BlockSpec((1,) + w_c.shape[1:], lambda b, p, d, j: (d, 0, 0)),
                  pl.BlockSpec((1,) + lev_c.shape[1:], lambda b, p, d, j: (d, 0, 0))],
        out_specs=pl.BlockSpec((r, wcols), outmap),
        out_shape=jax.ShapeDtypeStruct((t, BRANCH_W), BF16),
        scratch_shapes=[pltpu.VMEM((HGRN_HEADS_PER_STEP, A_DV, A_DK), F32),
                        pltpu.VMEM((s + l, wcols), F32)],
        compiler_params=_cparams("parallel", "parallel", "arbitrary", "arbitrary"),
        name="hgrn2",
    )(z_nr, z_f, z_nr, z_nr, lb.reshape(2, 1, BRANCH_W),
      jnp.tile(norm_g, A_HEADS).reshape(1, BRANCH_W), w_c, lev_c)


def _diff_attn_kernel(lam_ref, q_ref, *rest, n_seg, out_scale):
    kv = rest[:2 * n_seg]
    g_ref, o_ref = rest[2 * n_seg], rest[2 * n_seg + 1]
    q = q_ref[...]
    lane = lax.broadcasted_iota(jnp.int32, q.shape, 1)
    zero = jnp.zeros_like(q)
    qs = (jnp.where(lane < B_DH, q, zero), jnp.where(lane >= B_DH, q, zero))
    es, inv = [], []
    for comp in range(2):
        s = [_dot_nt(qs[comp], kv[2 * i][...]) for i in range(n_seg)]
        m = s[0].max(axis=-1, keepdims=True)
        for x in s[1:]:
            m = jnp.maximum(m, x.max(axis=-1, keepdims=True))
        e = [jnp.exp2(x - m) for x in s]
        tot = e[0].sum(axis=-1, keepdims=True)
        for x in e[1:]:
            tot = tot + x.sum(axis=-1, keepdims=True)
        es.append(e)
        inv.append(1.0 / tot)
    r0 = inv[0]
    r1 = inv[1] * lam_ref[0, 0]
    o = None
    for i in range(n_seg):
        wgt = (es[0][i] * r0 - es[1][i] * r1).astype(BF16)
        part = _dot(wgt, kv[2 * i + 1][...])
        o = part if o is None else o + part
    o_ref[...] = (o * _rms(o) * (g_ref[...] * out_scale)).astype(o_ref.dtype)


def _diff_attn(z_rb, z_nr, lam, subln_g, lambda_init, geo, prev=None):
    bsz, s, l = geo["B"], geo["S"], geo["L"]
    t = z_rb.shape[0]
    ctx0 = bsz * s // l
    kcol, vcol = B_HEADS, 3 * B_HEADS
    g2 = subln_g.reshape(1, 2 * B_DH)
    lam2 = lam.reshape(1, 1)
    smem = pl.BlockSpec(memory_space=pltpu.SMEM)
    if prev is None:
        tq = geo["tq_b"]
        nq = s // tq
        in_specs = [smem,
                    pl.BlockSpec((tq, LANES), lambda b, h, i: (b * nq + i, h)),
                    pl.BlockSpec((s, LANES), lambda b, h, i: (b, kcol + h)),
                    pl.BlockSpec((s, LANES), lambda b, h, i: (b, vcol + h)),
                    pl.BlockSpec((l, LANES), lambda b, h, i: (ctx0 + b, kcol + h)),
                    pl.BlockSpec((l, LANES), lambda b, h, i: (ctx0 + b, vcol + h)),
                    pl.BlockSpec((1, LANES), lambda b, h, i: (0, 0))]
        args = [lam2, z_rb, z_rb, z_nr, z_rb, z_nr, g2]
        out_spec = pl.BlockSpec((tq, LANES), lambda b, h, i: (b * nq + i, h))
        grid, n_seg, aliases = (bsz, B_HEADS, nq), 2, {}
    else:
        in_specs = [smem,
                    pl.BlockSpec((l, LANES), lambda b, h, i: (ctx0 + b, h)),
                    pl.BlockSpec((l, LANES), lambda b, h, i: (ctx0 + b, kcol + h)),
                    pl.BlockSpec((l, LANES), lambda b, h, i: (ctx0 + b, vcol + h)),
                    pl.BlockSpec((1, LANES), lambda b, h, i: (0, 0)),
                    pl.BlockSpec(memory_space=pl.ANY)]
        args = [lam2, z_rb, z_rb, z_nr, g2, prev]
        out_spec = pl.BlockSpec((l, LANES), lambda b, h, i: (ctx0 + b, h))
        grid, n_seg, aliases = (bsz, B_HEADS, 1), 1, {5: 0}

    def kern(*refs):
        if prev is not None:
            refs = refs[:5] + refs[6:]
        _diff_attn_kernel(*refs, n_seg=n_seg, out_scale=1.0 - lambda_init)

    return pl.pallas_call(
        kern,
        grid=grid,
        in_specs=in_specs,
        out_specs=out_spec,
        out_shape=jax.ShapeDtypeStruct((t, BRANCH_W), BF16),
        input_output_aliases=aliases,
        compiler_params=_cparams("parallel", "parallel", "arbitrary"),
        name="diff_attn_lat" if prev is None else "diff_attn_ctx",
    )(*args)


def _stack_heads(q_ref):
    return jnp.concatenate([q_ref[:, r * LANES:(r + 1) * LANES] for r in range(C_GROUP)], axis=0)


def _sink_column(sink_ref, g, rows_per_head):
    rid = lax.broadcasted_iota(jnp.int32, (C_GROUP * rows_per_head, 1), 0) // rows_per_head
    col = jnp.full(rid.shape, sink_ref[g * C_GROUP] * LOG2E, F32)
    for r in range(1, C_GROUP):
        col = jnp.where(rid == r, sink_ref[g * C_GROUP + r] * LOG2E, col)
    return col


def _sink_attend(scores, values, sink_col, o_ref, rows_per_head):
    m = sink_col
    for x in scores:
        m = jnp.maximum(m, x.max(axis=-1, keepdims=True))
    e = [jnp.exp2(x - m) for x in scores]
    tot = jnp.exp2(sink_col - m)
    for x in e:
        tot = tot + x.sum(axis=-1, keepdims=True)
    inv = 1.0 / tot
    o = None
    for x, v in zip(e, values):
        part = _dot((x * inv).astype(BF16), v)
        o = part if o is None else o + part
    for r in range(C_GROUP):
        o_ref[:, r * LANES:(r + 1) * LANES] = o[r * rows_per_head:(r + 1) * rows_per_head].astype(o_ref.dtype)


def _win_attn_kernel(sink_ref, q_ref, kc_ref, vc_ref, kp_ref, ko_ref, kn_ref, vp_ref, vo_ref, vn_ref,
                     o_ref, *, n_blk):
    g = pl.program_id(1)
    n = pl.program_id(2)
    q4 = _stack_heads(q_ref)
    rows = q4.shape[0]
    tq = lax.broadcasted_iota(jnp.int32, (rows, C_BLOCK), 0) % C_BLOCK
    kj = lax.broadcasted_iota(jnp.int32, (rows, C_BLOCK), 1)
    sp = jnp.where(jnp.logical_and(kj >= tq, n > 0), _dot_nt(q4, kp_ref[...]), NEG_BIG)
    sn = jnp.where(jnp.logical_and(kj <= tq, n < n_blk - 1), _dot_nt(q4, kn_ref[...]), NEG_BIG)
    scores = [_dot_nt(q4, kc_ref[...]), sp, _dot_nt(q4, ko_ref[...]), sn]
    values = [vc_ref[...], vp_ref[...], vo_ref[...], vn_ref[...]]
    _sink_attend(scores, values, _sink_column(sink_ref, g, C_BLOCK), o_ref, C_BLOCK)


def _win_attn_ctx_kernel(sink_ref, q_ref, kc_ref, vc_ref, prev_ref, o_ref):
    del prev_ref
    g = pl.program_id(1)
    q4 = _stack_heads(q_ref)
    rows_per_head = q_ref.shape[0]
    _sink_attend([_dot_nt(q4, kc_ref[...])], [vc_ref[...]],
                 _sink_column(sink_ref, g, rows_per_head), o_ref, rows_per_head)


def _win_attn(z_rc, z_nr, sink, geo, prev=None):
    bsz, s, l = geo["B"], geo["S"], geo["L"]
    t = z_rc.shape[0]
    n_blk = s // C_BLOCK
    qw = C_GROUP * LANES
    kcol = C_Q_HEADS
    vcol = (4 * BRANCH_W) // LANES
    ctx0 = bsz * s // l
    smem = pl.BlockSpec(memory_space=pltpu.SMEM)
    if prev is None:
        def band(colblk0, shift):
            def imap(b, g, n):
                return (b * n_blk + jnp.clip(n + shift, 0, n_blk - 1), colblk0 + g)
            return pl.BlockSpec((C_BLOCK, LANES), imap)

        in_specs = [smem,
                    pl.BlockSpec((C_BLOCK, qw), lambda b, g, n: (b * n_blk + n, g)),
                    pl.BlockSpec((l, LANES), lambda b, g, n: (ctx0 + b, kcol + g)),
                    pl.BlockSpec((l, LANES), lambda b, g, n: (ctx0 + b, vcol + g)),
                    band(kcol, -1), band(kcol, 0), band(kcol, 1),
                    band(vcol, -1), band(vcol, 0), band(vcol, 1)]
        args = [sink, z_rc, z_rc, z_nr, z_rc, z_rc, z_rc, z_nr, z_nr, z_nr]
        return pl.pallas_call(
            functools.partial(_win_attn_kernel, n_blk=n_blk),
            grid=(bsz, C_KV_HEADS, n_blk),
            in_specs=in_specs,
            out_specs=pl.BlockSpec((C_BLOCK, qw), lambda b, g, n: (b * n_blk + n, g)),
            out_shape=jax.ShapeDtypeStruct((t, BRANCH_W), BF16),
            compiler_params=_cparams("parallel", "parallel", "arbitrary"),
            name="win_attn_lat",
        )(*args)
    in_specs = [smem,
                pl.BlockSpec((l, qw), lambda b, g: (ctx0 + b, g)),
                pl.BlockSpec((l, LANES), lambda b, g: (ctx0 + b, kcol + g)),
                pl.BlockSpec((l, LANES), lambda b, g: (ctx0 + b, vcol + g)),
                pl.BlockSpec(memory_space=pl.ANY)]
    return pl.pallas_call(
        _win_attn_ctx_kernel,
        grid=(bsz, C_KV_HEADS),
        in_specs=in_specs,
        out_specs=pl.BlockSpec((l, qw), lambda b, g: (ctx0 + b, g)),
        out_shape=jax.ShapeDtypeStruct((t, BRANCH_W), BF16),
        input_output_aliases={4: 0},
        compiler_params=_cparams("parallel", "parallel"),
        name="win_attn_ctx",
    )(sink, z_rc, z_rc, z_nr, prev)


def _merge_kernel(a_ref, b_ref, c_ref, w_ref, ga_ref, gb_ref, gc_ref, o_ref):
    acc = ga_ref[...].astype(F32) * _dot(a_ref[...], w_ref[0])
    acc = acc + gb_ref[...].astype(F32) * _dot(b_ref[...], w_ref[1])
    acc = acc + gc_ref[...].astype(F32) * _dot(c_ref[...], w_ref[2])
    o_ref[...] = acc.astype(o_ref.dtype)


def _merge(oa, ob, oc, gates, w_branch, geo):
    t = oa.shape[0]
    tm = geo["tm_mm"]
    tn = 512
    nj = D_MODEL // tn
    br = pl.BlockSpec((tm, BRANCH_W), lambda i, j: (i, 0))

    def gate(n):
        return pl.BlockSpec((tm, tn), lambda i, j: (i, n * nj + j))

    return pl.pallas_call(
        _merge_kernel,
        grid=(t // tm, nj),
        in_specs=[br, br, br,
                  pl.BlockSpec((N_BRANCH, BRANCH_W, tn), lambda i, j: (0, 0, j)),
                  gate(0), gate(1), gate(2)],
        out_specs=pl.BlockSpec((tm, tn), lambda i, j: (i, j)),
        out_shape=jax.ShapeDtypeStruct((t, D_MODEL), BF16),
        compiler_params=_cparams("parallel", "arbitrary"),
        name="branch_merge",
    )(oa, ob, oc, w_branch, gates, gates, gates)


def _residual_epilogue(y, x_ref, gpost_ref, gate_ref, gpre_ref, sh_ref, sc_ref, xo_ref, ho_ref):
    x1 = x_ref[...] + gate_ref[0] * (y * _rms(y) * gpost_ref[...])
    xo_ref[...] = x1
    h = (x1 * _rms(x1) * gpre_ref[...]) * (1.0 + sc_ref[0]) + sh_ref[0]
    ho_ref[...] = h.astype(ho_ref.dtype)


def _outproj_kernel(m_ref, w_ref, *rest):
    _residual_epilogue(_dot(m_ref[...], w_ref[...]), *rest)


def _down_kernel(a_ref, w_ref, x_ref, gpost_ref, gate_ref, gpre_ref, sh_ref, sc_ref, xo_ref, ho_ref, acc_ref):
    k = pl.program_id(1)

    @pl.when(k == 0)
    def _():
        acc_ref[...] = jnp.zeros_like(acc_ref)

    acc_ref[...] += _dot(a_ref[...], w_ref[...])

    @pl.when(k == pl.num_programs(1) - 1)
    def _():
        _residual_epilogue(acc_ref[...], x_ref, gpost_ref, gate_ref, gpre_ref, sh_ref, sc_ref, xo_ref, ho_ref)


def _residual_specs(tm, d, midx, grid_rank):
    if grid_rank == 1:
        row = lambda i: (i, 0)
        const = lambda i: (0, 0)
        mod = lambda i: (midx(i), 0, 0)
    else:
        row = lambda i, k: (i, 0)
        const = lambda i, k: (0, 0)
        mod = lambda i, k: (midx(i), 0, 0)
    ins = [pl.BlockSpec((tm, d), row), pl.BlockSpec((1, d), const), pl.BlockSpec((1, 1, d), mod),
           pl.BlockSpec((1, d), const), pl.BlockSpec((1, 1, d), mod), pl.BlockSpec((1, 1, d), mod)]
    outs = [pl.BlockSpec((tm, d), row), pl.BlockSpec((tm, d), row)]
    return ins, outs


def _outproj(m, w, x, gpost, gate, gpre, sh, sc, geo):
    t, d = x.shape
    tm = geo["tm_row"]
    r_ins, r_outs = _residual_specs(tm, d, geo["midx"](tm), 1)
    return pl.pallas_call(
        _outproj_kernel,
        grid=(t // tm,),
        in_specs=[pl.BlockSpec((tm, d), lambda i: (i, 0)), pl.BlockSpec((d, d), lambda i: (0, 0))] + r_ins,
        out_specs=r_outs,
        out_shape=[jax.ShapeDtypeStruct((t, d), F32), jax.ShapeDtypeStruct((t, d), BF16)],
        compiler_params=_cparams("parallel"),
        name="outproj_residual",
    )(m, w, x, gpost.reshape(1, d), gate, gpre.reshape(1, d), sh, sc)


def _ffn_down(act, w, x, gpost, gate, gpre, sh, sc, geo):
    t, d = x.shape
    kf = act.shape[1]
    tm = geo["tm_row"]
    tk = _pick((512, 256, 128), kf)
    r_ins, r_outs = _residual_specs(tm, d, geo["midx"](tm), 2)
    return pl.pallas_call(
        _down_kernel,
        grid=(t // tm, kf // tk),
        in_specs=[pl.BlockSpec((tm, tk), lambda i, k: (i, k)), pl.BlockSpec((tk, d), lambda i, k: (k, 0))] + r_ins,
        out_specs=r_outs,
        out_shape=[jax.ShapeDtypeStruct((t, d), F32), jax.ShapeDtypeStruct((t, d), BF16)],
        scratch_shapes=[pltpu.VMEM((tm, d), F32)],
        compiler_params=_cparams("parallel", "arbitrary"),
        name="ffn_down_residual",
    )(act, w, x, gpost.reshape(1, d), gate, gpre.reshape(1, d), sh, sc)


HALO = 16


def _ffn_up_kernel(hp_ref, h_ref, hn_ref, wa_ref, wv_ref, cwa_ref, cwv_ref, cba_ref, cbv_ref, o_ref,
                   *, n_lat_blk, s, l):
    i = pl.program_id(0)
    tm = h_ref.shape[0]
    hext = jnp.concatenate([hp_ref[...], h_ref[...], hn_ref[...]], axis=0)
    row = lax.broadcasted_iota(jnp.int32, (tm, 1), 0)
    seq = jnp.where(i < n_lat_blk, s, l)
    base = jnp.where(i < n_lat_blk, i * tm, (i - n_lat_blk) * tm)
    pos = (base + row) % seq
    is_first = pos == 0
    is_last = pos == seq - 1

    def conv(w_ref, cw_ref, cb_ref):
        u = _dot(hext, w_ref[...])
        prev = jnp.where(is_first, 0.0, u[HALO - 1:HALO - 1 + tm])
        nxt = jnp.where(is_last, 0.0, u[HALO + 1:HALO + 1 + tm])
        cw = cw_ref[...]
        return prev * cw[0:1] + u[HALO:HALO + tm] * cw[1:2] + nxt * cw[2:3] + cb_ref[...]

    a = conv(wa_ref, cwa_ref, cba_ref)
    v = conv(wv_ref, cwv_ref, cbv_ref)
    o_ref[...] = (_silu(a) * v).astype(o_ref.dtype)


def _ffn_up(h, w_up, conv_w, conv_b, geo):
    t, d = h.shape
    tm = geo["tm_mm"]
    tn = _pick((512, 256, 128), D_FF)
    nj = D_FF // tn
    n_halo = t // HALO
    per = tm // HALO
    cb = conv_b.reshape(1, 2 * D_FF)
    kern = functools.partial(_ffn_up_kernel, n_lat_blk=geo["B"] * geo["S"] // tm, s=geo["S"], l=geo["L"])
    return pl.pallas_call(
        kern,
        grid=(t // tm, nj),
        in_specs=[pl.BlockSpec((HALO, d), lambda i, j: (jnp.maximum(i * per - 1, 0), 0)),
                  pl.BlockSpec((tm, d), lambda i, j: (i, 0)),
                  pl.BlockSpec((HALO, d), lambda i, j: (jnp.minimum((i + 1) * per, n_halo - 1), 0)),
                  pl.BlockSpec((d, tn), lambda i, j: (0, j)),
                  pl.BlockSpec((d, tn), lambda i, j: (0, nj + j)),
                  pl.BlockSpec((3, tn), lambda i, j: (0, j)),
                  pl.BlockSpec((3, tn), lambda i, j: (0, nj + j)),
                  pl.BlockSpec((1, tn), lambda i, j: (0, j)),
                  pl.BlockSpec((1, tn), lambda i, j: (0, nj + j))],
        out_specs=pl.BlockSpec((tm, tn), lambda i, j: (i, j)),
        out_shape=jax.ShapeDtypeStruct((t, D_FF), BF16),
        compiler_params=_cparams("parallel", "arbitrary"),
        name="ffn_up_conv",
    )(h, h, h, w_up, w_up, conv_w, conv_w, cb, cb)


def _rope_tables(s, head_dim, tm):
    quarter = head_dim // 4
    pos = np.arange(s)
    inv = np.power(np.float32(ROPE_BASE), -np.arange(quarter, dtype=np.float32) / quarter).astype(np.float32)
    row = (pos // GRID_W).astype(np.float32)[:, None] * inv
    col = (pos % GRID_W).astype(np.float32)[:, None] * inv
    ang = jnp.asarray(np.concatenate([row, row, col, col], axis=1))
    sign = np.concatenate([-np.ones(quarter), np.ones(quarter)] * 2).astype(np.float32)
    cos = jnp.cos(ang)
    sin = jnp.sin(ang) * sign
    reps = LANES // head_dim
    cos = jnp.tile(cos, (1, reps))
    sin = jnp.tile(sin, (1, reps))
    cos = jnp.concatenate([cos, jnp.ones((tm, LANES), F32)], axis=0)
    sin = jnp.concatenate([sin, jnp.zeros((tm, LANES), F32)], axis=0)
    return cos, sin, quarter


def _geometry(bsz, s, l):
    nl, nc = bsz * s, bsz * l

    def midx(tm):
        nlat, per = nl // tm, s // tm
        return lambda i: jnp.where(i < nlat, 1 + i // per, 0)

    def tabidx(tm):
        nlat, per = nl // tm, s // tm
        return lambda i: jnp.where(i < nlat, i % per, per)

    return {"B": bsz, "S": s, "L": l,
            "tm_mm": _pick((1024, 512, 256, 128), s, nc),
            "tm_row": _pick((512, 256, 128), s, nc),
            "tq_b": _pick((128,), s),
            "hgrn_rows": _pick((256, 128, 64), s, l),
            "midx": midx, "tabidx": tabidx}


def kernel(x, c, ctx, c_ctx, w_ada, b_ada, g_pre_mix, g_post_mix, g_pre_ffn, g_post_ffn, w_in, a_lb_logits, a_norm_g, b_lambda, b_subln_g, c_sink, w_branch, w_out, w_up, conv_w, conv_b, w_down):
    bsz, s, d = x.shape
    l = ctx.shape[1]
    depth = w_in.shape[0]
    assert d == D_MODEL and s % GRID_W == 0 and s % C_BLOCK == 0 and bsz + 1 <= MOD_ROWS
    geo = _geometry(bsz, s, l)
    nl = bsz * s

    xa = jnp.concatenate([x.reshape(nl, d), ctx.reshape(bsz * l, d)], axis=0)

    cc = jnp.concatenate([c_ctx[None], c, jnp.zeros((MOD_ROWS - 1 - bsz, d), F32)], axis=0)
    mod = _modulation(cc, w_ada, b_ada).reshape(depth, MOD_ROWS, 6, 1, d)

    def modv(layer, k):
        return mod[layer, :, k]

    p = jax.nn.softmax(a_lb_logits.astype(F32), axis=1)
    lower = jnp.cumsum(p, axis=1) - p[:, :1]
    hconsts = _hgrn_constants()

    tm = geo["tm_mm"]
    cos_b, sin_b, grp_b = _rope_tables(s, B_DH, tm)
    cos_c, sin_c, grp_c = _rope_tables(s, C_DH, tm)
    scale_b = jnp.concatenate([jnp.full((BRANCH_W,), B_DH ** -0.5 * LOG2E, F32),
                               jnp.ones((BRANCH_W,), F32)]).reshape(1, -1)
    scale_c = jnp.concatenate([jnp.full((BRANCH_W,), C_DH ** -0.5 * LOG2E, F32),
                               jnp.ones((C_KV_HEADS * C_DH,), F32)]).reshape(1, -1)

    e = np.cumsum((0, 1024, 1024, 1024, 1024, 1024, 1024, 1024, 1024, 1024, 256, 256, 6144))
    h = _prenorm(xa, g_pre_mix[0], modv(0, 0), modv(0, 1), geo)
    for layer in range(depth):
        wi = w_in[layer]
        w_nr = jnp.concatenate([wi[:, e[0]:e[1]], wi[:, e[3]:e[5]], wi[:, e[7]:e[8]], wi[:, e[10]:e[11]]],
                               axis=1).astype(BF16)
        w_f = wi[:, e[1]:e[3]].astype(BF16)
        w_rb = wi[:, e[5]:e[7]].astype(BF16)
        w_rc = wi[:, e[8]:e[10]].astype(BF16)
        w_g = wi[:, e[11]:e[12]].astype(BF16)
        lambda_init = 0.8 - 0.6 * math.exp(-0.3 * layer)
        lv = b_lambda[layer].astype(F32)
        lam = jnp.exp(jnp.sum(lv[0] * lv[1])) - jnp.exp(jnp.sum(lv[2] * lv[3])) + lambda_init

        z_nr = _project(h, w_nr, BF16, geo, tn_cands=(256, 128))
        z_f = _project(h, w_f, F32, geo)
        z_rb = _project(h, w_rb, BF16, geo, epi="rope", rope=(cos_b, sin_b, scale_b, grp_b))
        z_rc = _project(h, w_rc, BF16, geo, epi="rope", rope=(cos_c, sin_c, scale_c, grp_c),
                        tn_cands=(256, 128))
        gates = _project(h, w_g, BF16, geo, epi="sigmoid")

        oa = _hgrn(z_nr, z_f, lower[:, layer], a_norm_g[layer], hconsts, geo)
        ob = _diff_attn(z_rb, z_nr, lam, b_subln_g[layer], lambda_init, geo)
        ob = _diff_attn(z_rb, z_nr, lam, b_subln_g[layer], lambda_init, geo, prev=ob)
        oc = _win_attn(z_rc, z_nr, c_sink[layer].astype(F32), geo)
        oc = _win_attn(z_rc, z_nr, c_sink[layer].astype(F32), geo, prev=oc)

        m = _merge(oa, ob, oc, gates, w_branch[layer].astype(BF16), geo)
        xa, h = _outproj(m, w_out[layer].astype(BF16), xa, g_post_mix[layer], modv(layer, 2),
                         g_pre_ffn[layer], modv(layer, 3), modv(layer, 4), geo)
        act = _ffn_up(h, w_up[layer].astype(BF16), conv_w[layer], conv_b[layer], geo)
        nxt = min(layer + 1, depth - 1)
        xa, h = _ffn_down(act, w_down[layer].astype(BF16), xa, g_post_ffn[layer], modv(layer, 5),
                          g_pre_mix[nxt], modv(nxt, 0), modv(nxt, 1), geo)
    return xa[:nl].reshape(bsz, s, d)
```

```python
import functools
import math

import numpy as np
import jax
import jax.numpy as jnp
from jax import lax
from jax.experimental import pallas as pl
from jax.experimental.pallas import tpu as pltpu

F32 = jnp.float32
BF16 = jnp.bfloat16

D_MODEL = 2048
GRID_W = 64
EPS = 1e-6
ROPE_BASE = 10000.0
NEG_BIG = -1e30
BRANCH_W = D_MODEL // 2
A_HEADS = 8
A_DK = 128
A_DV = BRANCH_W // A_HEADS
B_HEADS = 8
B_DH = BRANCH_W // (2 * B_HEADS)
C_Q_HEADS = 8
C_KV_HEADS = 2
C_GROUP = C_Q_HEADS // C_KV_HEADS
C_DH = BRANCH_W // C_Q_HEADS
C_BLOCK = 128
D_FF = 5632
N_BRANCH = 3
LOG2E = 1.4426950408889634

LANES = 128
VMEM_LIMIT_BYTES = 56 * 1024 * 1024
MOD_ROWS = 16
HGRN_HEADS_PER_STEP = 4
ONES_ROWS = 16
SOFTMAX_COLS = 512


def _cparams(*sem):
    return pltpu.CompilerParams(dimension_semantics=sem, vmem_limit_bytes=VMEM_LIMIT_BYTES)


def _pick(cands, *dims):
    for c in cands:
        if all(d % c == 0 for d in dims):
            return c
    raise ValueError(f"no tile in {cands} divides {dims}")


def _dot(a, b):
    return jnp.dot(a, b, preferred_element_type=F32)


def _dot_nt(a, b):
    return lax.dot_general(a, b, (((1,), (1,)), ((), ())), preferred_element_type=F32)


def _dot_tn(a, b):
    return lax.dot_general(a, b, (((0,), (0,)), ((), ())), preferred_element_type=F32)


def _rms(x):
    return lax.rsqrt(jnp.mean(x * x, axis=-1, keepdims=True) + EPS)


def _silu(x):
    return x * jax.nn.sigmoid(x)


def _mod_kernel(c_ref, w_ref, b_ref, o_ref):
    s = _silu(c_ref[...])
    o_ref[0] = _dot(s.astype(BF16), w_ref[0].astype(BF16)) + b_ref[0]


def _modulation(cc, w_ada, b_ada):
    depth, d, n = w_ada.shape
    tn = _pick((1024, 512, 256, 128), n)
    return pl.pallas_call(
        _mod_kernel,
        grid=(depth, n // tn),
        in_specs=[pl.BlockSpec((MOD_ROWS, d), lambda l, j: (0, 0)),
                  pl.BlockSpec((1, d, tn), lambda l, j: (l, 0, j)),
                  pl.BlockSpec((1, 1, tn), lambda l, j: (l, 0, j))],
        out_specs=pl.BlockSpec((1, MOD_ROWS, tn), lambda l, j: (l, 0, j)),
        out_shape=jax.ShapeDtypeStruct((depth, MOD_ROWS, n), F32),
        compiler_params=_cparams("parallel", "parallel"),
        name="modulation",
    )(cc, w_ada, b_ada.reshape(depth, 1, n))


def _prenorm_kernel(x_ref, g_ref, sh_ref, sc_ref, h_ref):
    x = x_ref[...]
    h = (x * _rms(x) * g_ref[...]) * (1.0 + sc_ref[0]) + sh_ref[0]
    h_ref[...] = h.astype(h_ref.dtype)


def _prenorm(x, g, sh, sc, geo):
    t, d = x.shape
    tm = geo["tm_row"]
    midx = geo["midx"](tm)
    return pl.pallas_call(
        _prenorm_kernel,
        grid=(t // tm,),
        in_specs=[pl.BlockSpec((tm, d), lambda i: (i, 0)),
                  pl.BlockSpec((1, d), lambda i: (0, 0)),
                  pl.BlockSpec((1, 1, d), lambda i: (midx(i), 0, 0)),
                  pl.BlockSpec((1, 1, d), lambda i: (midx(i), 0, 0))],
        out_specs=pl.BlockSpec((tm, d), lambda i: (i, 0)),
        out_shape=jax.ShapeDtypeStruct((t, d), BF16),
        compiler_params=_cparams("parallel"),
        name="prenorm",
    )(x, g.reshape(1, d), sh, sc)


def _proj_kernel(h_ref, w_ref, *rest, epi, group):
    acc = _dot(h_ref[...], w_ref[...])
    if epi == "none":
        (o_ref,) = rest
        o_ref[...] = acc.astype(o_ref.dtype)
    elif epi == "sigmoid":
        (o_ref,) = rest
        o_ref[...] = jax.nn.sigmoid(acc).astype(o_ref.dtype)
    else:
        cos_ref, sin_ref, cs_ref, o_ref = rest
        tn = acc.shape[1]
        reps = tn // LANES
        cos = jnp.concatenate([cos_ref[...]] * reps, axis=1)
        sin = jnp.concatenate([sin_ref[...]] * reps, axis=1)
        lane = lax.broadcasted_iota(jnp.int32, acc.shape, 1)
        first = (lane % (2 * group)) < group
        partner = jnp.where(first, pltpu.roll(acc, tn - group, 1), pltpu.roll(acc, group, 1))
        o_ref[...] = ((acc * cos + partner * sin) * cs_ref[...]).astype(o_ref.dtype)


def _project(h, w, out_dtype, geo, epi="none", rope=None, tn_cands=(1024, 512, 256, 128)):
    t, d = h.shape
    n = w.shape[1]
    tm = geo["tm_mm"]
    tn = _pick(tn_cands, n)
    in_specs = [pl.BlockSpec((tm, d), lambda i, j: (i, 0)),
                pl.BlockSpec((d, tn), lambda i, j: (0, j))]
    args = [h, w]
    group = 0
    if epi == "rope":
        cos, sin, colscale, group = rope
        tab = geo["tabidx"](tm)
        in_specs += [pl.BlockSpec((tm, LANES), lambda i, j: (tab(i), 0)),
                     pl.BlockSpec((tm, LANES), lambda i, j: (tab(i), 0)),
                     pl.BlockSpec((1, tn), lambda i, j: (0, j))]
        args += [cos, sin, colscale]
    return pl.pallas_call(
        functools.partial(_proj_kernel, epi=epi, group=group),
        grid=(t // tm, n // tn),
        in_specs=in_specs,
        out_specs=pl.BlockSpec((tm, tn), lambda i, j: (i, j)),
        out_shape=jax.ShapeDtypeStruct((t, n), out_dtype),
        compiler_params=_cparams("parallel", "arbitrary"),
        name=f"proj_{epi}",
    )(*args)


def _hgrn_constants(c):
    nlev = int(math.log2(c))
    assert 1 << nlev == c
    t = np.arange(c)[:, None]
    u = np.arange(c)[None, :]
    w = np.zeros((2, (2 + nlev) * c + ONES_ROWS, c), np.float32)
    lev = np.full((2, c, c), -1, np.int32)
    for d in range(2):
        if d == 0:
            cum, rem = u <= t, u > t
        else:
            cum, rem = u >= t, u < t
        w[d, 0:c] = cum
        w[d, c:2 * c] = rem
        for l in range(nlev):
            h = 1 << l
            base = (t // (2 * h)) * (2 * h)
            if d == 0:
                m = base + h - 1
                e = np.where(t > m, (u > m) & (u <= t), (u > t) & (u <= m))
                pair = (t // (2 * h) == u // (2 * h)) & (t % (2 * h) >= h) & (u % (2 * h) < h)
            else:
                m = base + h
                e = np.where(t < m, (u >= t) & (u < m), (u >= m) & (u < t))
                pair = (t // (2 * h) == u // (2 * h)) & (t % (2 * h) < h) & (u % (2 * h) >= h)
            w[d, (2 + l) * c:(3 + l) * c] = e
            lev[d][pair] = l
        w[d, (2 + nlev) * c:] = 1.0
        lev[d][np.arange(c), np.arange(c)] = nlev
    return jnp.asarray(w, BF16), jnp.asarray(lev), nlev


def _hgrn_kernel(zq_ref, zf_ref, zi_ref, zg_ref, lb_ref, ng_ref, w_ref, lev_ref, o_ref,
                 st_ref, ofwd_ref, obuf_ref, *, nlev, n_ctx_blk, n_lat_blk):
    c = lev_ref.shape[1]
    rows = zq_ref.shape[0]
    n_chunks = rows // c
    d = pl.program_id(2)
    j = pl.program_id(3)

    @pl.when(j == 0)
    def _():
        st_ref[...] = jnp.zeros_like(st_ref)

    slot = jnp.where(d == 0, j,
                     jnp.where(j < n_ctx_blk, n_ctx_blk - 1 - j,
                               2 * n_ctx_blk + n_lat_blk - 1 - j))
    lb = lb_ref[0]
    w = w_ref[0]
    lev = lev_ref[0]
    masks = [lev == l for l in range(nlev + 1)]
    ng = ng_ref[...]

    def chunk(i, states):
        idx = jnp.where(d == 0, i, n_chunks - 1 - i)
        off = pl.multiple_of(idx * c, c)
        rs = pl.ds(off, c)
        zf = zf_ref[rs, :]
        sig = jax.nn.sigmoid(zf)
        f = lb + (1.0 - lb) * sig
        kk = (1.0 - lb) * (1.0 - sig)
        g = jnp.log(f)
        expo = _dot(w, g.astype(BF16))
        a = jnp.exp(expo)
        q = _silu(zq_ref[rs, :].astype(F32))
        v = zi_ref[rs, :]
        new_states = []
        for hh in range(HGRN_HEADS_PER_STEP):
            ls = slice(hh * LANES, (hh + 1) * LANES)
            qh, kh, vh = q[:, ls], kk[:, ls], v[:, ls]
            scores = jnp.where(masks[nlev], _dot_nt(qh.astype(BF16), kh.astype(BF16)), 0.0)
            for l in range(nlev):
                al = a[(2 + l) * c:(3 + l) * c, ls]
                p = _dot_nt((qh * al).astype(BF16), (kh * al).astype(BF16))
                scores = jnp.where(masks[l], p, scores)
            st = states[hh]
            o = _dot(scores.astype(BF16), vh) + _dot_nt((qh * a[0:c, ls]).astype(BF16), st.astype(BF16))
            ut = _dot_tn(vh, (kh * a[c:2 * c, ls]).astype(BF16))
            dec = a[(2 + nlev) * c:(2 + nlev) * c + 1, ls]
            new_states.append(st * dec + ut)
            obuf_ref[rs, ls] = o
        return new_states

    states = [st_ref[hh] for hh in range(HGRN_HEADS_PER_STEP)]
    for i in range(n_chunks):
        states = chunk(i, states)
    for hh in range(HGRN_HEADS_PER_STEP):
        st_ref[hh] = states[hh]

    srow = pl.ds(pl.multiple_of(slot * rows, rows), rows)

    @pl.when(d == 0)
    def _():
        ofwd_ref[srow, :] = obuf_ref[...]

    @pl.when(d == 1)
    def _():
        tot = ofwd_ref[srow, :] + obuf_ref[...]
        gate = _silu(zg_ref[...].astype(F32))
        fin = []
        for hh in range(HGRN_HEADS_PER_STEP):
            ls = slice(hh * LANES, (hh + 1) * LANES)
            th = tot[:, ls]
            fin.append(th * _rms(th) * ng[:, ls])
        o_ref[...] = (jnp.concatenate(fin, axis=1) * gate).astype(o_ref.dtype)


def _hgrn(z_nr, z_f, lb, norm_g, consts, geo):
    w_c, lev_c, nlev = consts
    bsz, s, l = geo["B"], geo["S"], geo["L"]
    t = z_nr.shape[0]
    r = geo["hgrn_rows"]
    nc, nl = l // r, s // r
    wcols = HGRN_HEADS_PER_STEP * LANES
    hp = BRANCH_W // wcols
    ctx0 = bsz * s // r

    def rowblk(b, d, j):
        slot = jnp.where(d == 0, j, jnp.where(j < nc, nc - 1 - j, 2 * nc + nl - 1 - j))
        return jnp.where(slot < nc, ctx0 + b * nc + slot, b * nl + slot - nc)

    def zmap(colblk0):
        return lambda b, p, d, j: (rowblk(b, d, j), colblk0 + p)

    def outmap(b, p, d, j):
        return (rowblk(b, 1, jnp.where(d == 0, 0, j)), p)

    kern = functools.partial(_hgrn_kernel, nlev=nlev, n_ctx_blk=nc, n_lat_blk=nl)
    return pl.pallas_call(
        kern,
        grid=(bsz, hp, 2, nc + nl),
        in_specs=[pl.BlockSpec((r, wcols), zmap(0)),
                  pl.BlockSpec((r, wcols), lambda b, p, d, j: (rowblk(b, d, j), d * hp + p)),
                  pl.BlockSpec((r, wcols), zmap(hp)),
                  pl.BlockSpec((r, wcols), zmap(2 * hp)),
                  pl.BlockSpec((1, 1, wcols), lambda b, p, d, j: (d, 0, p)),
                  pl.BlockSpec((1, wcols), lambda b, p, d, j: (0, p)),
                  pl.BlockSpec((1,) + w_c.shape[1:], lambda b, p, d, j: (d, 0, 0)),
                  pl.BlockSpec((1,) + lev_c.shape[1:], lambda b, p, d, j: (d, 0, 0))],
        out_specs=pl.BlockSpec((r, wcols), outmap),
        out_shape=jax.ShapeDtypeStruct((t, BRANCH_W), BF16),
        scratch_shapes=[pltpu.VMEM((HGRN_HEADS_PER_STEP, A_DV, A_DK), F32),
                        pltpu.VMEM((s + l, wcols), F32),
                        pltpu.VMEM((r, wcols), F32)],
        compiler_params=_cparams("parallel", "parallel", "arbitrary", "arbitrary"),
        name="hgrn2",
    )(z_nr, z_f, z_nr, z_nr, lb.reshape(2, 1, BRANCH_W),
      jnp.tile(norm_g, A_HEADS).reshape(1, BRANCH_W), w_c, lev_c)


def _diff_attn_kernel(lam_ref, q_ref, *rest, n_seg, out_scale):
    kv = rest[:2 * n_seg]
    g_ref, o_ref = rest[2 * n_seg], rest[2 * n_seg + 1]
    q = q_ref[...]
    lane = lax.broadcasted_iota(jnp.int32, q.shape, 1)
    zero = jnp.zeros_like(q)
    qs = (jnp.where(lane < B_DH, q, zero), jnp.where(lane >= B_DH, q, zero))
    es, inv = [], []
    for comp in range(2):
        s = [_dot_nt(qs[comp], kv[2 * i][...]) for i in range(n_seg)]
        m = s[0].max(axis=-1, keepdims=True)
        for x in s[1:]:
            m = jnp.maximum(m, x.max(axis=-1, keepdims=True))
        e = [jnp.exp2(x - m) for x in s]
        tot = e[0].sum(axis=-1, keepdims=True)
        for x in e[1:]:
            tot = tot + x.sum(axis=-1, keepdims=True)
        es.append(e)
        inv.append(1.0 / tot)
    r0 = inv[0]
    r1 = inv[1] * lam_ref[0, 0]
    o = None
    for i in range(n_seg):
        wgt = (es[0][i] * r0 - es[1][i] * r1).astype(BF16)
        part = _dot(wgt, kv[2 * i + 1][...])
        o = part if o is None else o + part
    o_ref[...] = (o * _rms(o) * (g_ref[...] * out_scale)).astype(o_ref.dtype)


def _diff_attn_pipe_kernel(lam_ref, q_ref, kl_ref, vl_ref, kc_ref, vc_ref, g_ref, o_ref,
                           s0_ref, s1_ref, m0_ref, m1_ref, e0_ref, e1_ref, wa_ref, wb_ref, ra_ref, rb_ref, *, tq, out_scale):
    s_len = kl_ref.shape[0]
    n = q_ref.shape[0] // tq
    lam = lam_ref[0, 0]
    gain = g_ref[...] * out_scale
    sa_ref, sb_ref = (s0_ref, m0_ref, e0_ref), (s1_ref, m1_ref, e1_ref)

    def rows(j):
        return pl.ds(pl.multiple_of(j * tq, tq), tq)

    def scores(j, sm_refs):
        s_ref, m_ref, _ = sm_refs
        q = q_ref[rows(j), :]
        lane = lax.broadcasted_iota(jnp.int32, q.shape, 1)
        zero = jnp.zeros_like(q)
        for comp, qc in enumerate((jnp.where(lane < B_DH, q, zero), jnp.where(lane >= B_DH, q, zero))):
            s_lat = _dot_nt(qc, kl_ref[...])
            s_ctx = _dot_nt(qc, kc_ref[...])
            s_ref[comp, :, 0:s_len] = s_lat
            s_ref[comp, :, s_len:] = s_ctx
            m_ref[comp] = jnp.maximum(s_lat.max(axis=-1, keepdims=True), s_ctx.max(axis=-1, keepdims=True))

    nk = s_len + kc_ref.shape[0]
    col_chunks = [(c0, min(c0 + SOFTMAX_COLS, nk)) for c0 in range(0, nk, SOFTMAX_COLS)]

    def softmax(sm_refs, w_ref, r_ref):
        s_ref, m_ref, e_ref = sm_refs
        ls = []
        for comp in range(2):
            m = m_ref[comp]
            acc = jnp.zeros((tq, LANES), F32)
            for c0, c1 in col_chunks:
                e = jnp.exp2(s_ref[comp, :, c0:c1] - m)
                for k in range(0, c1 - c0, LANES):
                    acc = acc + e[:, k:k + LANES]
                e_ref[comp, :, c0:c1] = e.astype(BF16)
            ls.append(acc.sum(axis=-1, keepdims=True))
        coef = (lam * ls[0] / ls[1]).astype(BF16)
        for c0, c1 in col_chunks:
            w_ref[:, c0:c1] = e_ref[0, :, c0:c1] - coef * e_ref[1, :, c0:c1]
        r_ref[...] = 1.0 / ls[0]

    def values(j, w_ref, r_ref):
        o = (_dot(w_ref[:, 0:s_len], vl_ref[...]) + _dot(w_ref[:, s_len:], vc_ref[...])) * r_ref[...]
        o_ref[rows(j), :] = (o * _rms(o) * gain).astype(o_ref.dtype)

    scores(0, sa_ref)
    scores(1, sb_ref)
    softmax(sa_ref, wa_ref, ra_ref)

    def pair(u, carry):
        t = 2 * u
        scores(t, sa_ref)
        softmax(sb_ref, wb_ref, rb_ref)
        values(t - 2, wa_ref, ra_ref)
        scores(t + 1, sb_ref)
        softmax(sa_ref, wa_ref, ra_ref)
        values(t - 1, wb_ref, rb_ref)
        return carry

    lax.fori_loop(1, n // 2, pair, 0)
    softmax(sb_ref, wb_ref, rb_ref)
    values(n - 2, wa_ref, ra_ref)
    values(n - 1, wb_ref, rb_ref)


def _diff_attn_lat(z_rb, z_nr, lam, subln_g, lambda_init, geo):
    bsz, s, l = geo["B"], geo["S"], geo["L"]
    t = z_rb.shape[0]
    ctx0 = bsz * s // l
    kcol, vcol = B_HEADS, 3 * B_HEADS
    tq = geo["tq_b"]
    assert (s // tq) % 2 == 0
    nk = s + l
    return pl.pallas_call(
        functools.partial(_diff_attn_pipe_kernel, tq=tq, out_scale=1.0 - lambda_init),
        grid=(bsz, B_HEADS),
        in_specs=[pl.BlockSpec(memory_space=pltpu.SMEM),
                  pl.BlockSpec((s, LANES), lambda b, h: (b, h)),
                  pl.BlockSpec((s, LANES), lambda b, h: (b, kcol + h)),
                  pl.BlockSpec((s, LANES), lambda b, h: (b, vcol + h)),
                  pl.BlockSpec((l, LANES), lambda b, h: (ctx0 + b, kcol + h)),
                  pl.BlockSpec((l, LANES), lambda b, h: (ctx0 + b, vcol + h)),
                  pl.BlockSpec((1, LANES), lambda b, h: (0, 0))],
        out_specs=pl.BlockSpec((s, LANES), lambda b, h: (b, h)),
        out_shape=jax.ShapeDtypeStruct((t, BRANCH_W), BF16),
        scratch_shapes=[pltpu.VMEM((2, tq, nk), F32), pltpu.VMEM((2, tq, nk), F32),
                        pltpu.VMEM((2, tq, 1), F32), pltpu.VMEM((2, tq, 1), F32),
                        pltpu.VMEM((2, tq, nk), BF16), pltpu.VMEM((2, tq, nk), BF16),
                        pltpu.VMEM((tq, nk), BF16), pltpu.VMEM((tq, nk), BF16),
                        pltpu.VMEM((tq, 1), F32), pltpu.VMEM((tq, 1), F32)],
        compiler_params=_cparams("parallel", "parallel"),
        name="diff_attn_lat",
    )(lam.reshape(1, 1), z_rb, z_rb, z_nr, z_rb, z_nr, subln_g.reshape(1, 2 * B_DH))


def _diff_attn(z_rb, z_nr, lam, subln_g, lambda_init, geo, prev=None):
    bsz, s, l = geo["B"], geo["S"], geo["L"]
    t = z_rb.shape[0]
    ctx0 = bsz * s // l
    kcol, vcol = B_HEADS, 3 * B_HEADS
    g2 = subln_g.reshape(1, 2 * B_DH)
    lam2 = lam.reshape(1, 1)
    if prev is None:
        return _diff_attn_lat(z_rb, z_nr, lam, subln_g, lambda_init, geo)

    def kern(lam_ref, q_ref, k_ref, v_ref, g_ref, prev_ref, o_ref):
        del prev_ref
        _diff_attn_kernel(lam_ref, q_ref, k_ref, v_ref, g_ref, o_ref, n_seg=1, out_scale=1.0 - lambda_init)

    return pl.pallas_call(
        kern,
        grid=(bsz, B_HEADS),
        in_specs=[pl.BlockSpec(memory_space=pltpu.SMEM),
                  pl.BlockSpec((l, LANES), lambda b, h: (ctx0 + b, h)),
                  pl.BlockSpec((l, LANES), lambda b, h: (ctx0 + b, kcol + h)),
                  pl.BlockSpec((l, LANES), lambda b, h: (ctx0 + b, vcol + h)),
                  pl.BlockSpec((1, LANES), lambda b, h: (0, 0)),
                  pl.BlockSpec(memory_space=pl.ANY)],
        out_specs=pl.BlockSpec((l, LANES), lambda b, h: (ctx0 + b, h)),
        out_shape=jax.ShapeDtypeStruct((t, BRANCH_W), BF16),
        input_output_aliases={5: 0},
        compiler_params=_cparams("parallel", "parallel"),
        name="diff_attn_ctx",
    )(lam2, z_rb, z_rb, z_nr, g2, prev)


def _stack_heads(q_ref):
    return jnp.concatenate([q_ref[:, r * LANES:(r + 1) * LANES] for r in range(C_GROUP)], axis=0)


def _sink_column(sink_ref, g, rows_per_head):
    rid = lax.broadcasted_iota(jnp.int32, (C_GROUP * rows_per_head, 1), 0) // rows_per_head
    col = jnp.full(rid.shape, sink_ref[g * C_GROUP] * LOG2E, F32)
    for r in range(1, C_GROUP):
        col = jnp.where(rid == r, sink_ref[g * C_GROUP + r] * LOG2E, col)
    return col


def _sink_attend(scores, values, sink_col, o_ref, rows_per_head):
    m = sink_col
    for x in scores:
        m = jnp.maximum(m, x.max(axis=-1, keepdims=True))
    e = [jnp.exp2(x - m) for x in scores]
    tot = jnp.exp2(sink_col - m)
    for x in e:
        tot = tot + x.sum(axis=-1, keepdims=True)
    inv = 1.0 / tot
    o = None
    for x, v in zip(e, values):
        part = _dot((x * inv).astype(BF16), v)
        o = part if o is None else o + part
    for r in range(C_GROUP):
        o_ref[:, r * LANES:(r + 1) * LANES] = o[r * rows_per_head:(r + 1) * rows_per_head].astype(o_ref.dtype)


def _win_attn_kernel(sink_ref, q_ref, kc_ref, vc_ref, kp_ref, ko_ref, kn_ref, vp_ref, vo_ref, vn_ref,
                     o_ref, *, n_blk):
    g = pl.program_id(1)
    n = pl.program_id(2)
    q4 = _stack_heads(q_ref)
    rows = q4.shape[0]
    tq = lax.broadcasted_iota(jnp.int32, (rows, C_BLOCK), 0) % C_BLOCK
    kj = lax.broadcasted_iota(jnp.int32, (rows, C_BLOCK), 1)
    sp = jnp.where(jnp.logical_and(kj >= tq, n > 0), _dot_nt(q4, kp_ref[...]), NEG_BIG)
    sn = jnp.where(jnp.logical_and(kj <= tq, n < n_blk - 1), _dot_nt(q4, kn_ref[...]), NEG_BIG)
    scores = [_dot_nt(q4, kc_ref[...]), sp, _dot_nt(q4, ko_ref[...]), sn]
    values = [vc_ref[...], vp_ref[...], vo_ref[...], vn_ref[...]]
    _sink_attend(scores, values, _sink_column(sink_ref, g, C_BLOCK), o_ref, C_BLOCK)


def _win_attn_ctx_kernel(sink_ref, q_ref, kc_ref, vc_ref, prev_ref, o_ref):
    del prev_ref
    g = pl.program_id(1)
    q4 = _stack_heads(q_ref)
    rows_per_head = q_ref.shape[0]
    _sink_attend([_dot_nt(q4, kc_ref[...])], [vc_ref[...]],
                 _sink_column(sink_ref, g, rows_per_head), o_ref, rows_per_head)


def _win_attn(z_rc, z_nr, sink, geo, prev=None):
    bsz, s, l = geo["B"], geo["S"], geo["L"]
    t = z_rc.shape[0]
    n_blk = s // C_BLOCK
    qw = C_GROUP * LANES
    kcol = C_Q_HEADS
    vcol = (4 * BRANCH_W) // LANES
    ctx0 = bsz * s // l
    smem = pl.BlockSpec(memory_space=pltpu.SMEM)
    if prev is None:
        def band(colblk0, shift):
            def imap(b, g, n):
                return (b * n_blk + jnp.clip(n + shift, 0, n_blk - 1), colblk0 + g)
            return pl.BlockSpec((C_BLOCK, LANES), imap)

        in_specs = [smem,
                    pl.BlockSpec((C_BLOCK, qw), lambda b, g, n: (b * n_blk + n, g)),
                    pl.BlockSpec((l, LANES), lambda b, g, n: (ctx0 + b, kcol + g)),
                    pl.BlockSpec((l, LANES), lambda b, g, n: (ctx0 + b, vcol + g)),
                    band(kcol, -1), band(kcol, 0), band(kcol, 1),
                    band(vcol, -1), band(vcol, 0), band(vcol, 1)]
        args = [sink, z_rc, z_rc, z_nr, z_rc, z_rc, z_rc, z_nr, z_nr, z_nr]
        return pl.pallas_call(
            functools.partial(_win_attn_kernel, n_blk=n_blk),
            grid=(bsz, C_KV_HEADS, n_blk),
            in_specs=in_specs,
            out_specs=pl.BlockSpec((C_BLOCK, qw), lambda b, g, n: (b * n_blk + n, g)),
            out_shape=jax.ShapeDtypeStruct((t, BRANCH_W), BF16),
            compiler_params=_cparams("parallel", "parallel", "arbitrary"),
            name="win_attn_lat",
        )(*args)
    in_specs = [smem,
                pl.BlockSpec((l, qw), lambda b, g: (ctx0 + b, g)),
                pl.BlockSpec((l, LANES), lambda b, g: (ctx0 + b, kcol + g)),
                pl.BlockSpec((l, LANES), lambda b, g: (ctx0 + b, vcol + g)),
                pl.BlockSpec(memory_space=pl.ANY)]
    return pl.pallas_call(
        _win_attn_ctx_kernel,
        grid=(bsz, C_KV_HEADS),
        in_specs=in_specs,
        out_specs=pl.BlockSpec((l, qw), lambda b, g: (ctx0 + b, g)),
        out_shape=jax.ShapeDtypeStruct((t, BRANCH_W), BF16),
        input_output_aliases={4: 0},
        compiler_params=_cparams("parallel", "parallel"),
        name="win_attn_ctx",
    )(sink, z_rc, z_rc, z_nr, prev)


def _merge_kernel(a_ref, b_ref, c_ref, w_ref, ga_ref, gb_ref, gc_ref, o_ref):
    acc = ga_ref[...].astype(F32) * _dot(a_ref[...], w_ref[0])
    acc = acc + gb_ref[...].astype(F32) * _dot(b_ref[...], w_ref[1])
    acc = acc + gc_ref[...].astype(F32) * _dot(c_ref[...], w_ref[2])
    o_ref[...] = acc.astype(o_ref.dtype)


def _merge(oa, ob, oc, gates, w_branch, geo):
    t = oa.shape[0]
    tm = geo["tm_mm"]
    tn = 512
    nj = D_MODEL // tn
    br = pl.BlockSpec((tm, BRANCH_W), lambda i, j: (i, 0))

    def gate(n):
        return pl.BlockSpec((tm, tn), lambda i, j: (i, n * nj + j))

    return pl.pallas_call(
        _merge_kernel,
        grid=(t // tm, nj),
        in_specs=[br, br, br,
                  pl.BlockSpec((N_BRANCH, BRANCH_W, tn), lambda i, j: (0, 0, j)),
                  gate(0), gate(1), gate(2)],
        out_specs=pl.BlockSpec((tm, tn), lambda i, j: (i, j)),
        out_shape=jax.ShapeDtypeStruct((t, D_MODEL), BF16),
        compiler_params=_cparams("parallel", "arbitrary"),
        name="branch_merge",
    )(oa, ob, oc, w_branch, gates, gates, gates)


def _residual_epilogue(y, x_ref, gpost_ref, gate_ref, gpre_ref, sh_ref, sc_ref, xo_ref, ho_ref):
    x1 = x_ref[...] + gate_ref[0] * (y * _rms(y) * gpost_ref[...])
    xo_ref[...] = x1
    h = (x1 * _rms(x1) * gpre_ref[...]) * (1.0 + sc_ref[0]) + sh_ref[0]
    ho_ref[...] = h.astype(ho_ref.dtype)


def _outproj_kernel(m_ref, w_ref, *rest):
    _residual_epilogue(_dot(m_ref[...], w_ref[...]), *rest)


def _down_kernel(a_ref, w_ref, x_ref, gpost_ref, gate_ref, gpre_ref, sh_ref, sc_ref, xo_ref, ho_ref, acc_ref):
    k = pl.program_id(1)

    @pl.when(k == 0)
    def _():
        acc_ref[...] = jnp.zeros_like(acc_ref)

    acc_ref[...] += _dot(a_ref[...], w_ref[...])

    @pl.when(k == pl.num_programs(1) - 1)
    def _():
        _residual_epilogue(acc_ref[...], x_ref, gpost_ref, gate_ref, gpre_ref, sh_ref, sc_ref, xo_ref, ho_ref)


def _residual_specs(tm, d, midx, grid_rank):
    if grid_rank == 1:
        row = lambda i: (i, 0)
        const = lambda i: (0, 0)
        mod = lambda i: (midx(i), 0, 0)
    else:
        row = lambda i, k: (i, 0)
        const = lambda i, k: (0, 0)
        mod = lambda i, k: (midx(i), 0, 0)
    ins = [pl.BlockSpec((tm, d), row), pl.BlockSpec((1, d), const), pl.BlockSpec((1, 1, d), mod),
           pl.BlockSpec((1, d), const), pl.BlockSpec((1, 1, d), mod), pl.BlockSpec((1, 1, d), mod)]
    outs = [pl.BlockSpec((tm, d), row), pl.BlockSpec((tm, d), row)]
    return ins, outs


def _outproj(m, w, x, gpost, gate, gpre, sh, sc, geo):
    t, d = x.shape
    tm = geo["tm_row"]
    r_ins, r_outs = _residual_specs(tm, d, geo["midx"](tm), 1)
    return pl.pallas_call(
        _outproj_kernel,
        grid=(t // tm,),
        in_specs=[pl.BlockSpec((tm, d), lambda i: (i, 0)), pl.BlockSpec((d, d), lambda i: (0, 0))] + r_ins,
        out_specs=r_outs,
        out_shape=[jax.ShapeDtypeStruct((t, d), F32), jax.ShapeDtypeStruct((t, d), BF16)],
        compiler_params=_cparams("parallel"),
        name="outproj_residual",
    )(m, w, x, gpost.reshape(1, d), gate, gpre.reshape(1, d), sh, sc)


def _ffn_down(act, w, x, gpost, gate, gpre, sh, sc, geo):
    t, d = x.shape
    kf = act.shape[1]
    tm = geo["tm_row"]
    tk = _pick((1408, 512, 256, 128), kf)
    r_ins, r_outs = _residual_specs(tm, d, geo["midx"](tm), 2)
    return pl.pallas_call(
        _down_kernel,
        grid=(t // tm, kf // tk),
        in_specs=[pl.BlockSpec((tm, tk), lambda i, k: (i, k)), pl.BlockSpec((tk, d), lambda i, k: (k, 0))] + r_ins,
        out_specs=r_outs,
        out_shape=[jax.ShapeDtypeStruct((t, d), F32), jax.ShapeDtypeStruct((t, d), BF16)],
        scratch_shapes=[pltpu.VMEM((tm, d), F32)],
        compiler_params=_cparams("parallel", "arbitrary"),
        name="ffn_down_residual",
    )(act, w, x, gpost.reshape(1, d), gate, gpre.reshape(1, d), sh, sc)


HALO = 16


def _ffn_up_kernel(hp_ref, h_ref, hn_ref, wa_ref, wv_ref, cwa_ref, cwv_ref, cba_ref, cbv_ref, o_ref,
                   *, n_lat_blk, s, l):
    i = pl.program_id(0)
    tm = h_ref.shape[0]
    hext = jnp.concatenate([hp_ref[...], h_ref[...], hn_ref[...]], axis=0)
    row = lax.broadcasted_iota(jnp.int32, (tm, 1), 0)

    def edges(blk, seq):
        if seq >= tm:
            per = seq // tm
            return tm - 1, blk % per == 0, blk % per == per - 1
        assert seq & (seq - 1) == 0, "short sequences must have power-of-two length"
        return seq - 1, True, True

    is_lat = i < n_lat_blk
    pm_l, bf_l, bl_l = edges(i, s)
    pm_c, bf_c, bl_c = edges(i - n_lat_blk, l)
    pm = jnp.where(is_lat, pm_l, pm_c)
    is_first = jnp.logical_and((row & pm) == 0, jnp.where(is_lat, bf_l, bf_c))
    is_last = jnp.logical_and((row & pm) == pm, jnp.where(is_lat, bl_l, bl_c))

    def conv(w_ref, cw_ref, cb_ref):
        u = _dot(hext, w_ref[...])
        prev = jnp.where(is_first, 0.0, u[HALO - 1:HALO - 1 + tm])
        nxt = jnp.where(is_last, 0.0, u[HALO + 1:HALO + 1 + tm])
        cw = cw_ref[...]
        return prev * cw[0:1] + u[HALO:HALO + tm] * cw[1:2] + nxt * cw[2:3] + cb_ref[...]

    a = conv(wa_ref, cwa_ref, cba_ref)
    v = conv(wv_ref, cwv_ref, cbv_ref)
    o_ref[...] = (_silu(a) * v).astype(o_ref.dtype)


def _ffn_up(h, w_up, conv_w, conv_b, geo):
    t, d = h.shape
    tm = geo["tm_mm"]
    tn = _pick((512, 256, 128), D_FF)
    nj = D_FF // tn
    n_halo = t // HALO
    per = tm // HALO
    cb = conv_b.reshape(1, 2 * D_FF)
    kern = functools.partial(_ffn_up_kernel, n_lat_blk=geo["B"] * geo["S"] // tm, s=geo["S"], l=geo["L"])
    return pl.pallas_call(
        kern,
        grid=(t // tm, nj),
        in_specs=[pl.BlockSpec((HALO, d), lambda i, j: (jnp.maximum(i * per - 1, 0), 0)),
                  pl.BlockSpec((tm, d), lambda i, j: (i, 0)),
                  pl.BlockSpec((HALO, d), lambda i, j: (jnp.minimum((i + 1) * per, n_halo - 1), 0)),
                  pl.BlockSpec((d, tn), lambda i, j: (0, j)),
                  pl.BlockSpec((d, tn), lambda i, j: (0, nj + j)),
                  pl.BlockSpec((3, tn), lambda i, j: (0, j)),
                  pl.BlockSpec((3, tn), lambda i, j: (0, nj + j)),
                  pl.BlockSpec((1, tn), lambda i, j: (0, j)),
                  pl.BlockSpec((1, tn), lambda i, j: (0, nj + j))],
        out_specs=pl.BlockSpec((tm, tn), lambda i, j: (i, j)),
        out_shape=jax.ShapeDtypeStruct((t, D_FF), BF16),
        compiler_params=_cparams("parallel", "arbitrary"),
        name="ffn_up_conv",
    )(h, h, h, w_up, w_up, conv_w, conv_w, cb, cb)


def _rope_tables(s, head_dim, tm):
    quarter = head_dim // 4
    pos = np.arange(s)
    inv = np.power(np.float32(ROPE_BASE), -np.arange(quarter, dtype=np.float32) / quarter).astype(np.float32)
    row = (pos // GRID_W).astype(np.float32)[:, None] * inv
    col = (pos % GRID_W).astype(np.float32)[:, None] * inv
    ang = jnp.asarray(np.concatenate([row, row, col, col], axis=1))
    sign = np.concatenate([-np.ones(quarter), np.ones(quarter)] * 2).astype(np.float32)
    cos = jnp.cos(ang)
    sin = jnp.sin(ang) * sign
    reps = LANES // head_dim
    cos = jnp.tile(cos, (1, reps))
    sin = jnp.tile(sin, (1, reps))
    cos = jnp.concatenate([cos, jnp.ones((tm, LANES), F32)], axis=0)
    sin = jnp.concatenate([sin, jnp.zeros((tm, LANES), F32)], axis=0)
    return cos, sin, quarter


def _geometry(bsz, s, l):
    nl, nc = bsz * s, bsz * l

    def midx(tm):
        nlat, per = nl // tm, s // tm
        return lambda i: jnp.where(i < nlat, 1 + i // per, 0)

    def tabidx(tm):
        nlat, per = nl // tm, s // tm
        return lambda i: jnp.where(i < nlat, i % per, per)

    return {"B": bsz, "S": s, "L": l,
            "tm_mm": _pick((1024, 512, 256, 128), s, nc),
            "tm_row": _pick((512, 256, 128), s, nc),
            "tq_b": _pick((128, 64), s // 2),
            "hgrn_rows": _pick((256, 128, 64), s, l),
            "midx": midx, "tabidx": tabidx}


def kernel(x, c, ctx, c_ctx, w_ada, b_ada, g_pre_mix, g_post_mix, g_pre_ffn, g_post_ffn, w_in, a_lb_logits, a_norm_g, b_lambda, b_subln_g, c_sink, w_branch, w_out, w_up, conv_w, conv_b, w_down):
    bsz, s, d = x.shape
    l = ctx.shape[1]
    depth = w_in.shape[0]
    assert d == D_MODEL and s % GRID_W == 0 and s % C_BLOCK == 0 and bsz + 1 <= MOD_ROWS
    geo = _geometry(bsz, s, l)
    nl = bsz * s

    xa = jnp.concatenate([x.reshape(nl, d), ctx.reshape(bsz * l, d)], axis=0)

    cc = jnp.concatenate([c_ctx[None], c, jnp.zeros((MOD_ROWS - 1 - bsz, d), F32)], axis=0)
    mod = _modulation(cc, w_ada, b_ada).reshape(depth, MOD_ROWS, 6, 1, d)

    def modv(layer, k):
        return mod[layer, :, k]

    p = jax.nn.softmax(a_lb_logits.astype(F32), axis=1)
    lower = jnp.cumsum(p, axis=1) - p[:, :1]
    hconsts = _hgrn_constants(geo["hgrn_rows"])

    tm = geo["tm_mm"]
    cos_b, sin_b, grp_b = _rope_tables(s, B_DH, tm)
    cos_c, sin_c, grp_c = _rope_tables(s, C_DH, tm)
    scale_b = jnp.concatenate([jnp.full((BRANCH_W,), B_DH ** -0.5 * LOG2E, F32),
                               jnp.ones((BRANCH_W,), F32)]).reshape(1, -1)
    scale_c = jnp.concatenate([jnp.full((BRANCH_W,), C_DH ** -0.5 * LOG2E, F32),
                               jnp.ones((C_KV_HEADS * C_DH,), F32)]).reshape(1, -1)

    e = np.cumsum((0, 1024, 1024, 1024, 1024, 1024, 1024, 1024, 1024, 1024, 256, 256, 6144))
    h = _prenorm(xa, g_pre_mix[0], modv(0, 0), modv(0, 1), geo)
    for layer in range(depth):
        wi = w_in[layer]
        w_nr = jnp.concatenate([wi[:, e[0]:e[1]], wi[:, e[3]:e[5]], wi[:, e[7]:e[8]], wi[:, e[10]:e[11]]],
                               axis=1).astype(BF16)
        w_f = wi[:, e[1]:e[3]].astype(BF16)
        w_rb = wi[:, e[5]:e[7]].astype(BF16)
        w_rc = wi[:, e[8]:e[10]].astype(BF16)
        w_g = wi[:, e[11]:e[12]].astype(BF16)
        lambda_init = 0.8 - 0.6 * math.exp(-0.3 * layer)
        lv = b_lambda[layer].astype(F32)
        lam = jnp.exp(jnp.sum(lv[0] * lv[1])) - jnp.exp(jnp.sum(lv[2] * lv[3])) + lambda_init

        z_nr = _project(h, w_nr, BF16, geo, tn_cands=(2176, 256, 128))
        z_f = _project(h, w_f, F32, geo)
        z_rb = _project(h, w_rb, BF16, geo, epi="rope", rope=(cos_b, sin_b, scale_b, grp_b))
        z_rc = _project(h, w_rc, BF16, geo, epi="rope", rope=(cos_c, sin_c, scale_c, grp_c),
                        tn_cands=(1280, 256, 128))
        gates = _project(h, w_g, BF16, geo, epi="sigmoid")

        oa = _hgrn(z_nr, z_f, lower[:, layer], a_norm_g[layer], hconsts, geo)
        ob = _diff_attn(z_rb, z_nr, lam, b_subln_g[layer], lambda_init, geo)
        ob = _diff_attn(z_rb, z_nr, lam, b_subln_g[layer], lambda_init, geo, prev=ob)
        oc = _win_attn(z_rc, z_nr, c_sink[layer].astype(F32), geo)
        oc = _win_attn(z_rc, z_nr, c_sink[layer].astype(F32), geo, prev=oc)

        m = _merge(oa, ob, oc, gates, w_branch[layer].astype(BF16), geo)
        xa, h = _outproj(m, w_out[layer].astype(BF16), xa, g_post_mix[layer], modv(layer, 2),
                         g_pre_ffn[layer], modv(layer, 3), modv(layer, 4), geo)
        act = _ffn_up(h, w_up[layer].astype(BF16), conv_w[layer], conv_b[layer], geo)
        nxt = min(layer + 1, depth - 1)
        xa, h = _ffn_down(act, w_down[layer].astype(BF16), xa, g_post_ffn[layer], modv(layer, 5),
                          g_pre_mix[nxt], modv(nxt, 0), modv(nxt, 1), geo)
    return xa[:nl].reshape(bsz, s, d)
```

```python
import functools
import math

import numpy as np
import jax
import jax.numpy as jnp
from jax import lax
from jax.experimental import pallas as pl
from jax.experimental.pallas import tpu as pltpu

F32 = jnp.float32
BF16 = jnp.bfloat16

D_MODEL = 2048
GRID_W = 64
EPS = 1e-6
ROPE_BASE = 10000.0
NEG_BIG = -1e30
BRANCH_W = D_MODEL // 2
A_HEADS = 8
A_DK = 128
A_DV = BRANCH_W // A_HEADS
B_HEADS = 8
B_DH = BRANCH_W // (2 * B_HEADS)
C_Q_HEADS = 8
C_KV_HEADS = 2
C_GROUP = C_Q_HEADS // C_KV_HEADS
C_DH = BRANCH_W // C_Q_HEADS
C_BLOCK = 128
D_FF = 5632
N_BRANCH = 3
LOG2E = 1.4426950408889634

LANES = 128
VMEM_LIMIT_BYTES = 56 * 1024 * 1024
MOD_ROWS = 16
HGRN_HEADS_PER_STEP = 4
ONES_ROWS = 16
SOFTMAX_COLS = 512


def _cparams(*sem):
    return pltpu.CompilerParams(dimension_semantics=sem, vmem_limit_bytes=VMEM_LIMIT_BYTES)


def _pick(cands, *dims):
    for c in cands:
        if all(d % c == 0 for d in dims):
            return c
    raise ValueError(f"no tile in {cands} divides {dims}")


def _dot(a, b):
    return jnp.dot(a, b, preferred_element_type=F32)


def _dot_nt(a, b):
    return lax.dot_general(a, b, (((1,), (1,)), ((), ())), preferred_element_type=F32)


def _dot_tn(a, b):
    return lax.dot_general(a, b, (((0,), (0,)), ((), ())), preferred_element_type=F32)


def _rms(x):
    return lax.rsqrt(jnp.mean(x * x, axis=-1, keepdims=True) + EPS)


def _silu(x):
    return x * jax.nn.sigmoid(x)


def _mod_kernel(c_ref, w_ref, b_ref, o_ref):
    s = _silu(c_ref[...])
    o_ref[0] = _dot(s.astype(BF16), w_ref[0].astype(BF16)) + b_ref[0]


def _modulation(cc, w_ada, b_ada):
    depth, d, n = w_ada.shape
    tn = _pick((1024, 512, 256, 128), n)
    return pl.pallas_call(
        _mod_kernel,
        grid=(depth, n // tn),
        in_specs=[pl.BlockSpec((MOD_ROWS, d), lambda l, j: (0, 0)),
                  pl.BlockSpec((1, d, tn), lambda l, j: (l, 0, j)),
                  pl.BlockSpec((1, 1, tn), lambda l, j: (l, 0, j))],
        out_specs=pl.BlockSpec((1, MOD_ROWS, tn), lambda l, j: (l, 0, j)),
        out_shape=jax.ShapeDtypeStruct((depth, MOD_ROWS, n), F32),
        compiler_params=_cparams("parallel", "parallel"),
        name="modulation",
    )(cc, w_ada, b_ada.reshape(depth, 1, n))


def _prenorm_kernel(x_ref, g_ref, sh_ref, sc_ref, h_ref):
    x = x_ref[...]
    h = (x * _rms(x) * g_ref[...]) * (1.0 + sc_ref[0]) + sh_ref[0]
    h_ref[...] = h.astype(h_ref.dtype)


def _prenorm(x, g, sh, sc, geo):
    t, d = x.shape
    tm = geo["tm_row"]
    midx = geo["midx"](tm)
    return pl.pallas_call(
        _prenorm_kernel,
        grid=(t // tm,),
        in_specs=[pl.BlockSpec((tm, d), lambda i: (i, 0)),
                  pl.BlockSpec((1, d), lambda i: (0, 0)),
                  pl.BlockSpec((1, 1, d), lambda i: (midx(i), 0, 0)),
                  pl.BlockSpec((1, 1, d), lambda i: (midx(i), 0, 0))],
        out_specs=pl.BlockSpec((tm, d), lambda i: (i, 0)),
        out_shape=jax.ShapeDtypeStruct((t, d), BF16),
        compiler_params=_cparams("parallel"),
        name="prenorm",
    )(x, g.reshape(1, d), sh, sc)


def _proj_kernel(h_ref, w_ref, *rest, epi, group):
    acc = _dot(h_ref[...], w_ref[...])
    if epi == "none":
        (o_ref,) = rest
        o_ref[...] = acc.astype(o_ref.dtype)
    elif epi == "sigmoid":
        (o_ref,) = rest
        o_ref[...] = jax.nn.sigmoid(acc).astype(o_ref.dtype)
    else:
        cos_ref, sin_ref, cs_ref, o_ref = rest
        tn = acc.shape[1]
        reps = tn // LANES
        cos = jnp.concatenate([cos_ref[...]] * reps, axis=1)
        sin = jnp.concatenate([sin_ref[...]] * reps, axis=1)
        lane = lax.broadcasted_iota(jnp.int32, acc.shape, 1)
        first = (lane % (2 * group)) < group
        partner = jnp.where(first, pltpu.roll(acc, tn - group, 1), pltpu.roll(acc, group, 1))
        o_ref[...] = ((acc * cos + partner * sin) * cs_ref[...]).astype(o_ref.dtype)


def _project(h, w, out_dtype, geo, epi="none", rope=None, tn_cands=(1024, 512, 256, 128)):
    t, d = h.shape
    n = w.shape[1]
    tm = geo["tm_mm"]
    tn = _pick(tn_cands, n)
    in_specs = [pl.BlockSpec((tm, d), lambda i, j: (i, 0)),
                pl.BlockSpec((d, tn), lambda i, j: (0, j))]
    args = [h, w]
    group = 0
    if epi == "rope":
        cos, sin, colscale, group = rope
        tab = geo["tabidx"](tm)
        in_specs += [pl.BlockSpec((tm, LANES), lambda i, j: (tab(i), 0)),
                     pl.BlockSpec((tm, LANES), lambda i, j: (tab(i), 0)),
                     pl.BlockSpec((1, tn), lambda i, j: (0, j))]
        args += [cos, sin, colscale]
    return pl.pallas_call(
        functools.partial(_proj_kernel, epi=epi, group=group),
        grid=(t // tm, n // tn),
        in_specs=in_specs,
        out_specs=pl.BlockSpec((tm, tn), lambda i, j: (i, j)),
        out_shape=jax.ShapeDtypeStruct((t, n), out_dtype),
        compiler_params=_cparams("parallel", "arbitrary"),
        name=f"proj_{epi}",
    )(*args)


def _hgrn_constants(c):
    nlev = int(math.log2(c))
    assert 1 << nlev == c
    t = np.arange(c)[:, None]
    u = np.arange(c)[None, :]
    w = np.zeros((2, (2 + nlev) * c + ONES_ROWS, c), np.float32)
    lev = np.full((2, c, c), -1, np.int32)
    for d in range(2):
        if d == 0:
            cum, rem = u <= t, u > t
        else:
            cum, rem = u >= t, u < t
        w[d, 0:c] = cum
        w[d, c:2 * c] = rem
        for l in range(nlev):
            h = 1 << l
            base = (t // (2 * h)) * (2 * h)
            if d == 0:
                m = base + h - 1
                e = np.where(t > m, (u > m) & (u <= t), (u > t) & (u <= m))
                pair = (t // (2 * h) == u // (2 * h)) & (t % (2 * h) >= h) & (u % (2 * h) < h)
            else:
                m = base + h
                e = np.where(t < m, (u >= t) & (u < m), (u >= m) & (u < t))
                pair = (t // (2 * h) == u // (2 * h)) & (t % (2 * h) < h) & (u % (2 * h) >= h)
            w[d, (2 + l) * c:(3 + l) * c] = e
            lev[d][pair] = l
        w[d, (2 + nlev) * c:] = 1.0
        lev[d][np.arange(c), np.arange(c)] = nlev
    return jnp.asarray(w, BF16), jnp.asarray(lev), nlev


def _hgrn_kernel(zq_ref, zf_ref, zi_ref, zg_ref, lb_ref, ng_ref, w_ref, lev_ref, o_ref,
                 st_ref, ofwd_ref, obuf_ref, *, nlev, n_ctx_blk, n_lat_blk):
    c = lev_ref.shape[1]
    rows = zq_ref.shape[0]
    n_chunks = rows // c
    d = pl.program_id(2)
    j = pl.program_id(3)

    @pl.when(j == 0)
    def _():
        st_ref[...] = jnp.zeros_like(st_ref)

    slot = jnp.where(d == 0, j,
                     jnp.where(j < n_ctx_blk, n_ctx_blk - 1 - j,
                               2 * n_ctx_blk + n_lat_blk - 1 - j))
    lb = lb_ref[0]
    w = w_ref[0]
    lev = lev_ref[0]
    masks = [lev == l for l in range(nlev + 1)]
    ng = ng_ref[...]

    def chunk(i, states):
        idx = jnp.where(d == 0, i, n_chunks - 1 - i)
        off = pl.multiple_of(idx * c, c)
        rs = pl.ds(off, c)
        zf = zf_ref[rs, :]
        sig = jax.nn.sigmoid(zf)
        f = lb + (1.0 - lb) * sig
        kk = (1.0 - lb) * (1.0 - sig)
        g = jnp.log(f)
        expo = _dot(w, g.astype(BF16))
        a = jnp.exp(expo)
        q = _silu(zq_ref[rs, :].astype(F32))
        v = zi_ref[rs, :]
        new_states = []
        for hh in range(HGRN_HEADS_PER_STEP):
            ls = slice(hh * LANES, (hh + 1) * LANES)
            qh, kh, vh = q[:, ls], kk[:, ls], v[:, ls]
            scores = jnp.where(masks[nlev], _dot_nt(qh.astype(BF16), kh.astype(BF16)), 0.0)
            for l in range(nlev):
                al = a[(2 + l) * c:(3 + l) * c, ls]
                p = _dot_nt((qh * al).astype(BF16), (kh * al).astype(BF16))
                scores = jnp.where(masks[l], p, scores)
            st = states[hh]
            o = _dot(scores.astype(BF16), vh) + _dot_nt((qh * a[0:c, ls]).astype(BF16), st.astype(BF16))
            ut = _dot_tn(vh, (kh * a[c:2 * c, ls]).astype(BF16))
            dec = a[(2 + nlev) * c:(2 + nlev) * c + 1, ls]
            new_states.append(st * dec + ut)
            obuf_ref[rs, ls] = o
        return new_states

    states = [st_ref[hh] for hh in range(HGRN_HEADS_PER_STEP)]
    for i in range(n_chunks):
        states = chunk(i, states)
    for hh in range(HGRN_HEADS_PER_STEP):
        st_ref[hh] = states[hh]

    srow = pl.ds(pl.multiple_of(slot * rows, rows), rows)

    @pl.when(d == 0)
    def _():
        ofwd_ref[srow, :] = obuf_ref[...]

    @pl.when(d == 1)
    def _():
        tot = ofwd_ref[srow, :] + obuf_ref[...]
        gate = _silu(zg_ref[...].astype(F32))
        fin = []
        for hh in range(HGRN_HEADS_PER_STEP):
            ls = slice(hh * LANES, (hh + 1) * LANES)
            th = tot[:, ls]
            fin.append(th * _rms(th) * ng[:, ls])
        o_ref[...] = (jnp.concatenate(fin, axis=1) * gate).astype(o_ref.dtype)


def _hgrn(z_nr, z_f, lb, norm_g, consts, geo):
    w_c, lev_c, nlev = consts
    bsz, s, l = geo["B"], geo["S"], geo["L"]
    t = z_nr.shape[0]
    r = geo["hgrn_rows"]
    nc, nl = l // r, s // r
    wcols = HGRN_HEADS_PER_STEP * LANES
    hp = BRANCH_W // wcols
    ctx0 = bsz * s // r

    def rowblk(b, d, j):
        slot = jnp.where(d == 0, j, jnp.where(j < nc, nc - 1 - j, 2 * nc + nl - 1 - j))
        return jnp.where(slot < nc, ctx0 + b * nc + slot, b * nl + slot - nc)

    def zmap(colblk0):
        return lambda b, p, d, j: (rowblk(b, d, j), colblk0 + p)

    def outmap(b, p, d, j):
        return (rowblk(b, 1, jnp.where(d == 0, 0, j)), p)

    kern = functools.partial(_hgrn_kernel, nlev=nlev, n_ctx_blk=nc, n_lat_blk=nl)
    return pl.pallas_call(
        kern,
        grid=(bsz, hp, 2, nc + nl),
        in_specs=[pl.BlockSpec((r, wcols), zmap(0)),
                  pl.BlockSpec((r, wcols), lambda b, p, d, j: (rowblk(b, d, j), d * hp + p)),
                  pl.BlockSpec((r, wcols), zmap(hp)),
                  pl.BlockSpec((r, wcols), zmap(2 * hp)),
                  pl.BlockSpec((1, 1, wcols), lambda b, p, d, j: (d, 0, p)),
                  pl.BlockSpec((1, wcols), lambda b, p, d, j: (0, p)),
                  pl.BlockSpec((1,) + w_c.shape[1:], lambda b, p, d, j: (d, 0, 0)),
                  pl.BlockSpec((1,) + lev_c.shape[1:], lambda b, p, d, j: (d, 0, 0))],
        out_specs=pl.BlockSpec((r, wcols), outmap),
        out_shape=jax.ShapeDtypeStruct((t, BRANCH_W), BF16),
        scratch_shapes=[pltpu.VMEM((HGRN_HEADS_PER_STEP, A_DV, A_DK), F32),
                        pltpu.VMEM((s + l, wcols), F32),
                        pltpu.VMEM((r, wcols), F32)],
        compiler_params=_cparams("parallel", "parallel", "arbitrary", "arbitrary"),
        name="hgrn2",
    )(z_nr, z_f, z_nr, z_nr, lb.reshape(2, 1, BRANCH_W),
      jnp.tile(norm_g, A_HEADS).reshape(1, BRANCH_W), w_c, lev_c)


def _diff_attn_kernel(lam_ref, q_ref, *rest, n_seg, out_scale):
    kv = rest[:2 * n_seg]
    g_ref, o_ref = rest[2 * n_seg], rest[2 * n_seg + 1]
    q = q_ref[...]
    lane = lax.broadcasted_iota(jnp.int32, q.shape, 1)
    zero = jnp.zeros_like(q)
    qs = (jnp.where(lane < B_DH, q, zero), jnp.where(lane >= B_DH, q, zero))
    es, inv = [], []
    for comp in range(2):
        s = [_dot_nt(qs[comp], kv[2 * i][...]) for i in range(n_seg)]
        m = s[0].max(axis=-1, keepdims=True)
        for x in s[1:]:
            m = jnp.maximum(m, x.max(axis=-1, keepdims=True))
        e = [jnp.exp2(x - m) for x in s]
        tot = e[0].sum(axis=-1, keepdims=True)
        for x in e[1:]:
            tot = tot + x.sum(axis=-1, keepdims=True)
        es.append(e)
        inv.append(1.0 / tot)
    r0 = inv[0]
    r1 = inv[1] * lam_ref[0, 0]
    o = None
    for i in range(n_seg):
        wgt = (es[0][i] * r0 - es[1][i] * r1).astype(BF16)
        part = _dot(wgt, kv[2 * i + 1][...])
        o = part if o is None else o + part
    o_ref[...] = (o * _rms(o) * (g_ref[...] * out_scale)).astype(o_ref.dtype)


def _diff_attn_pipe_kernel(lam_ref, q_ref, kl_ref, vl_ref, kc_ref, vc_ref, g_ref, o_ref,
                           s0_ref, s1_ref, m0_ref, m1_ref, e0_ref, e1_ref, wa_ref, wb_ref, ra_ref, rb_ref, *, tq, out_scale):
    s_len = kl_ref.shape[0]
    n = q_ref.shape[0] // tq
    lam = lam_ref[0, 0]
    gain = g_ref[...] * out_scale
    sa_ref, sb_ref = (s0_ref, m0_ref, e0_ref), (s1_ref, m1_ref, e1_ref)

    def rows(j):
        return pl.ds(pl.multiple_of(j * tq, tq), tq)

    def scores(j, sm_refs):
        s_ref, m_ref, _ = sm_refs
        q = q_ref[rows(j), :]
        lane = lax.broadcasted_iota(jnp.int32, q.shape, 1)
        zero = jnp.zeros_like(q)
        for comp, qc in enumerate((jnp.where(lane < B_DH, q, zero), jnp.where(lane >= B_DH, q, zero))):
            s_lat = _dot_nt(qc, kl_ref[...])
            s_ctx = _dot_nt(qc, kc_ref[...])
            s_ref[comp, :, 0:s_len] = s_lat
            s_ref[comp, :, s_len:] = s_ctx
            m_ref[comp] = jnp.maximum(s_lat.max(axis=-1, keepdims=True), s_ctx.max(axis=-1, keepdims=True))

    nk = s_len + kc_ref.shape[0]
    col_chunks = [(c0, min(c0 + SOFTMAX_COLS, nk)) for c0 in range(0, nk, SOFTMAX_COLS)]

    def softmax(sm_refs, w_ref, r_ref):
        s_ref, m_ref, e_ref = sm_refs
        ls = []
        for comp in range(2):
            m = m_ref[comp]
            acc = jnp.zeros((tq, LANES), F32)
            for c0, c1 in col_chunks:
                e = jnp.exp2(s_ref[comp, :, c0:c1] - m)
                for k in range(0, c1 - c0, LANES):
                    acc = acc + e[:, k:k + LANES]
                e_ref[comp, :, c0:c1] = e.astype(BF16)
            ls.append(acc.sum(axis=-1, keepdims=True))
        coef = (lam * ls[0] / ls[1]).astype(BF16)
        for c0, c1 in col_chunks:
            w_ref[:, c0:c1] = e_ref[0, :, c0:c1] - coef * e_ref[1, :, c0:c1]
        r_ref[...] = 1.0 / ls[0]

    def values(j, w_ref, r_ref):
        o = (_dot(w_ref[:, 0:s_len], vl_ref[...]) + _dot(w_ref[:, s_len:], vc_ref[...])) * r_ref[...]
        o_ref[rows(j), :] = (o * _rms(o) * gain).astype(o_ref.dtype)

    scores(0, sa_ref)
    scores(1, sb_ref)
    softmax(sa_ref, wa_ref, ra_ref)

    def pair(u, carry):
        t = 2 * u
        values(t - 2, wa_ref, ra_ref)
        softmax(sb_ref, wb_ref, rb_ref)
        scores(t, sa_ref)
        values(t - 1, wb_ref, rb_ref)
        softmax(sa_ref, wa_ref, ra_ref)
        scores(t + 1, sb_ref)
        return carry

    lax.fori_loop(1, n // 2, pair, 0)
    softmax(sb_ref, wb_ref, rb_ref)
    values(n - 2, wa_ref, ra_ref)
    values(n - 1, wb_ref, rb_ref)


def _diff_attn_lat(z_rb, z_nr, lam, subln_g, lambda_init, geo):
    bsz, s, l = geo["B"], geo["S"], geo["L"]
    t = z_rb.shape[0]
    ctx0 = bsz * s // l
    kcol, vcol = B_HEADS, 3 * B_HEADS
    tq = geo["tq_b"]
    assert (s // tq) % 2 == 0
    nk = s + l
    return pl.pallas_call(
        functools.partial(_diff_attn_pipe_kernel, tq=tq, out_scale=1.0 - lambda_init),
        grid=(bsz, B_HEADS),
        in_specs=[pl.BlockSpec(memory_space=pltpu.SMEM),
                  pl.BlockSpec((s, LANES), lambda b, h: (b, h)),
                  pl.BlockSpec((s, LANES), lambda b, h: (b, kcol + h)),
                  pl.BlockSpec((s, LANES), lambda b, h: (b, vcol + h)),
                  pl.BlockSpec((l, LANES), lambda b, h: (ctx0 + b, kcol + h)),
                  pl.BlockSpec((l, LANES), lambda b, h: (ctx0 + b, vcol + h)),
                  pl.BlockSpec((1, LANES), lambda b, h: (0, 0))],
        out_specs=pl.BlockSpec((s, LANES), lambda b, h: (b, h)),
        out_shape=jax.ShapeDtypeStruct((t, BRANCH_W), BF16),
        scratch_shapes=[pltpu.VMEM((2, tq, nk), F32), pltpu.VMEM((2, tq, nk), F32),
                        pltpu.VMEM((2, tq, 1), F32), pltpu.VMEM((2, tq, 1), F32),
                        pltpu.VMEM((2, tq, nk), BF16), pltpu.VMEM((2, tq, nk), BF16),
                        pltpu.VMEM((tq, nk), BF16), pltpu.VMEM((tq, nk), BF16),
                        pltpu.VMEM((tq, 1), F32), pltpu.VMEM((tq, 1), F32)],
        compiler_params=_cparams("parallel", "parallel"),
        name="diff_attn_lat",
    )(lam.reshape(1, 1), z_rb, z_rb, z_nr, z_rb, z_nr, subln_g.reshape(1, 2 * B_DH))


def _diff_attn(z_rb, z_nr, lam, subln_g, lambda_init, geo, prev=None):
    bsz, s, l = geo["B"], geo["S"], geo["L"]
    t = z_rb.shape[0]
    ctx0 = bsz * s // l
    kcol, vcol = B_HEADS, 3 * B_HEADS
    g2 = subln_g.reshape(1, 2 * B_DH)
    lam2 = lam.reshape(1, 1)
    if prev is None:
        return _diff_attn_lat(z_rb, z_nr, lam, subln_g, lambda_init, geo)

    def kern(lam_ref, q_ref, k_ref, v_ref, g_ref, prev_ref, o_ref):
        del prev_ref
        _diff_attn_kernel(lam_ref, q_ref, k_ref, v_ref, g_ref, o_ref, n_seg=1, out_scale=1.0 - lambda_init)

    return pl.pallas_call(
        kern,
        grid=(bsz, B_HEADS),
        in_specs=[pl.BlockSpec(memory_space=pltpu.SMEM),
                  pl.BlockSpec((l, LANES), lambda b, h: (ctx0 + b, h)),
                  pl.BlockSpec((l, LANES), lambda b, h: (ctx0 + b, kcol + h)),
                  pl.BlockSpec((l, LANES), lambda b, h: (ctx0 + b, vcol + h)),
                  pl.BlockSpec((1, LANES), lambda b, h: (0, 0)),
                  pl.BlockSpec(memory_space=pl.ANY)],
        out_specs=pl.BlockSpec((l, LANES), lambda b, h: (ctx0 + b, h)),
        out_shape=jax.ShapeDtypeStruct((t, BRANCH_W), BF16),
        input_output_aliases={5: 0},
        compiler_params=_cparams("parallel", "parallel"),
        name="diff_attn_ctx",
    )(lam2, z_rb, z_rb, z_nr, g2, prev)


def _stack_heads(q_ref, r0, rows):
    return jnp.concatenate([q_ref[r0:r0 + rows, r * LANES:(r + 1) * LANES] for r in range(C_GROUP)], axis=0)


def _sink_column(sink_ref, g, rows_per_head):
    rid = lax.broadcasted_iota(jnp.int32, (C_GROUP * rows_per_head, 1), 0) // rows_per_head
    col = jnp.full(rid.shape, sink_ref[g * C_GROUP] * LOG2E, F32)
    for r in range(1, C_GROUP):
        col = jnp.where(rid == r, sink_ref[g * C_GROUP + r] * LOG2E, col)
    return col


def _lane_groups(x):
    return [x[:, k:k + LANES] for k in range(0, x.shape[1], LANES)]


def _sink_attend(scores, values, sink_col, o_ref, r0, rows_per_head):
    groups = [grp for x in scores for grp in _lane_groups(x)]
    mvec = groups[0]
    for grp in groups[1:]:
        mvec = jnp.maximum(mvec, grp)
    m = jnp.maximum(sink_col, mvec.max(axis=-1, keepdims=True))
    e = [jnp.exp2(x - m) for x in scores]
    egroups = [grp for x in e for grp in _lane_groups(x)]
    svec = egroups[0]
    for grp in egroups[1:]:
        svec = svec + grp
    inv = 1.0 / (jnp.exp2(sink_col - m) + svec.sum(axis=-1, keepdims=True))
    o = None
    for x, v in zip(e, values):
        part = _dot(x.astype(BF16), v)
        o = part if o is None else o + part
    o = o * inv
    for r in range(C_GROUP):
        o_ref[r0:r0 + rows_per_head, r * LANES:(r + 1) * LANES] = (
            o[r * rows_per_head:(r + 1) * rows_per_head].astype(o_ref.dtype))


def _win_attn_kernel(sink_ref, q_ref, kc_ref, vc_ref, kp_ref, km_ref, kn_ref, vp_ref, vm_ref, vn_ref,
                     o_ref, *, n_step):
    g = pl.program_id(1)
    n = pl.program_id(2)
    n_sub = q_ref.shape[0] // C_BLOCK
    rows = C_GROUP * C_BLOCK
    tq = lax.broadcasted_iota(jnp.int32, (rows, C_BLOCK), 0) % C_BLOCK
    kj = lax.broadcasted_iota(jnp.int32, (rows, C_BLOCK), 1)
    sink_col = _sink_column(sink_ref, g, C_BLOCK)

    def blk(ref, a):
        return ref[a * C_BLOCK:(a + 1) * C_BLOCK, :]

    kband = [kp_ref[...]] + [blk(km_ref, a) for a in range(n_sub)] + [kn_ref[...]]
    vband = [vp_ref[...]] + [blk(vm_ref, a) for a in range(n_sub)] + [vn_ref[...]]
    for a in range(n_sub):
        q4 = _stack_heads(q_ref, a * C_BLOCK, C_BLOCK)
        has_prev = True if a > 0 else n > 0
        has_next = True if a < n_sub - 1 else n < n_step - 1
        sp = jnp.where(jnp.logical_and(kj >= tq, has_prev), _dot_nt(q4, kband[a]), NEG_BIG)
        sn = jnp.where(jnp.logical_and(kj <= tq, has_next), _dot_nt(q4, kband[a + 2]), NEG_BIG)
        scores = [_dot_nt(q4, kc_ref[...]), sp, _dot_nt(q4, kband[a + 1]), sn]
        values = [vc_ref[...], vband[a], vband[a + 1], vband[a + 2]]
        _sink_attend(scores, values, sink_col, o_ref, a * C_BLOCK, C_BLOCK)


def _win_attn_ctx_kernel(sink_ref, q_ref, kc_ref, vc_ref, prev_ref, o_ref):
    del prev_ref
    g = pl.program_id(1)
    rows_per_head = q_ref.shape[0]
    q4 = _stack_heads(q_ref, 0, rows_per_head)
    _sink_attend([_dot_nt(q4, kc_ref[...])], [vc_ref[...]],
                 _sink_column(sink_ref, g, rows_per_head), o_ref, 0, rows_per_head)


def _win_attn(z_rc, z_nr, sink, geo, prev=None):
    bsz, s, l = geo["B"], geo["S"], geo["L"]
    t = z_rc.shape[0]
    n_blk = s // C_BLOCK
    qw = C_GROUP * LANES
    kcol = C_Q_HEADS
    vcol = (4 * BRANCH_W) // LANES
    ctx0 = bsz * s // l
    smem = pl.BlockSpec(memory_space=pltpu.SMEM)
    if prev is None:
        qb = geo["win_rows"]
        sub = qb // C_BLOCK
        n_step = s // qb

        def edge(colblk0, shift):
            def imap(b, g, n):
                return (b * n_blk + jnp.clip(n * sub + shift, 0, n_blk - 1), colblk0 + g)
            return pl.BlockSpec((C_BLOCK, LANES), imap)

        def mid(colblk0):
            return pl.BlockSpec((qb, LANES), lambda b, g, n: (b * n_step + n, colblk0 + g))

        in_specs = [smem,
                    pl.BlockSpec((qb, qw), lambda b, g, n: (b * n_step + n, g)),
                    pl.BlockSpec((l, LANES), lambda b, g, n: (ctx0 + b, kcol + g)),
                    pl.BlockSpec((l, LANES), lambda b, g, n: (ctx0 + b, vcol + g)),
                    edge(kcol, -1), mid(kcol), edge(kcol, sub),
                    edge(vcol, -1), mid(vcol), edge(vcol, sub)]
        args = [sink, z_rc, z_rc, z_nr, z_rc, z_rc, z_rc, z_nr, z_nr, z_nr]
        return pl.pallas_call(
            functools.partial(_win_attn_kernel, n_step=n_step),
            grid=(bsz, C_KV_HEADS, n_step),
            in_specs=in_specs,
            out_specs=pl.BlockSpec((qb, qw), lambda b, g, n: (b * n_step + n, g)),
            out_shape=jax.ShapeDtypeStruct((t, BRANCH_W), BF16),
            compiler_params=_cparams("parallel", "parallel", "arbitrary"),
            name="win_attn_lat",
        )(*args)
    in_specs = [smem,
                pl.BlockSpec((l, qw), lambda b, g: (ctx0 + b, g)),
                pl.BlockSpec((l, LANES), lambda b, g: (ctx0 + b, kcol + g)),
                pl.BlockSpec((l, LANES), lambda b, g: (ctx0 + b, vcol + g)),
                pl.BlockSpec(memory_space=pl.ANY)]
    return pl.pallas_call(
        _win_attn_ctx_kernel,
        grid=(bsz, C_KV_HEADS),
        in_specs=in_specs,
        out_specs=pl.BlockSpec((l, qw), lambda b, g: (ctx0 + b, g)),
        out_shape=jax.ShapeDtypeStruct((t, BRANCH_W), BF16),
        input_output_aliases={4: 0},
        compiler_params=_cparams("parallel", "parallel"),
        name="win_attn_ctx",
    )(sink, z_rc, z_rc, z_nr, prev)


def _merge_kernel(a_ref, b_ref, c_ref, w_ref, ga_ref, gb_ref, gc_ref, o_ref):
    acc = ga_ref[...].astype(F32) * _dot(a_ref[...], w_ref[0])
    acc = acc + gb_ref[...].astype(F32) * _dot(b_ref[...], w_ref[1])
    acc = acc + gc_ref[...].astype(F32) * _dot(c_ref[...], w_ref[2])
    o_ref[...] = acc.astype(o_ref.dtype)


def _merge(oa, ob, oc, gates, w_branch, geo, rows):
    t = rows
    tm = geo["tm_mm"]
    tn = 512
    nj = D_MODEL // tn
    br = pl.BlockSpec((tm, BRANCH_W), lambda i, j: (i, 0))

    def gate(n):
        return pl.BlockSpec((tm, tn), lambda i, j: (i, n * nj + j))

    return pl.pallas_call(
        _merge_kernel,
        grid=(t // tm, nj),
        in_specs=[br, br, br,
                  pl.BlockSpec((N_BRANCH, BRANCH_W, tn), lambda i, j: (0, 0, j)),
                  gate(0), gate(1), gate(2)],
        out_specs=pl.BlockSpec((tm, tn), lambda i, j: (i, j)),
        out_shape=jax.ShapeDtypeStruct((t, D_MODEL), BF16),
        compiler_params=_cparams("parallel", "arbitrary"),
        name="branch_merge",
    )(oa, ob, oc, w_branch, gates, gates, gates)


def _residual_epilogue(y, x_ref, gpost_ref, gate_ref, gpre_ref, sh_ref, sc_ref, xo_ref, ho_ref):
    x1 = x_ref[...] + gate_ref[0] * (y * _rms(y) * gpost_ref[...])
    xo_ref[...] = x1
    h = (x1 * _rms(x1) * gpre_ref[...]) * (1.0 + sc_ref[0]) + sh_ref[0]
    ho_ref[...] = h.astype(ho_ref.dtype)


def _outproj_kernel(m_ref, w_ref, *rest):
    _residual_epilogue(_dot(m_ref[...], w_ref[...]), *rest)


def _down_kernel(a_ref, w_ref, x_ref, gpost_ref, gate_ref, gpre_ref, sh_ref, sc_ref, xo_ref, ho_ref, y_ref):
    j = pl.program_id(1)
    nj = y_ref.shape[0]
    y_ref[j] = _dot(a_ref[...], w_ref[...])

    @pl.when(j == nj - 1)
    def _():
        y = jnp.concatenate([y_ref[jj] for jj in range(nj)], axis=1)
        _residual_epilogue(y, x_ref, gpost_ref, gate_ref, gpre_ref, sh_ref, sc_ref, xo_ref, ho_ref)


def _residual_specs(tm, d, midx, grid_rank):
    if grid_rank == 1:
        row = lambda i: (i, 0)
        const = lambda i: (0, 0)
        mod = lambda i: (midx(i), 0, 0)
    else:
        row = lambda i, k: (i, 0)
        const = lambda i, k: (0, 0)
        mod = lambda i, k: (midx(i), 0, 0)
    ins = [pl.BlockSpec((tm, d), row), pl.BlockSpec((1, d), const), pl.BlockSpec((1, 1, d), mod),
           pl.BlockSpec((1, d), const), pl.BlockSpec((1, 1, d), mod), pl.BlockSpec((1, 1, d), mod)]
    outs = [pl.BlockSpec((tm, d), row), pl.BlockSpec((tm, d), row)]
    return ins, outs


def _outproj(m, w, x, gpost, gate, gpre, sh, sc, geo, rows):
    t, d = rows, x.shape[1]
    tm = geo["tm_row"]
    r_ins, r_outs = _residual_specs(tm, d, geo["midx"](tm), 1)
    return pl.pallas_call(
        _outproj_kernel,
        grid=(t // tm,),
        in_specs=[pl.BlockSpec((tm, d), lambda i: (i, 0)), pl.BlockSpec((d, d), lambda i: (0, 0))] + r_ins,
        out_specs=r_outs,
        out_shape=[jax.ShapeDtypeStruct((t, d), F32), jax.ShapeDtypeStruct((t, d), BF16)],
        compiler_params=_cparams("parallel"),
        name="outproj_residual",
    )(m, w, x, gpost.reshape(1, d), gate, gpre.reshape(1, d), sh, sc)


def _ffn_down(act, w, x, gpost, gate, gpre, sh, sc, geo, rows):
    t, d = rows, x.shape[1]
    kf = act.shape[1]
    tm = geo["tm_row"]
    tn = _pick((512, 256, 128), d)
    r_ins, r_outs = _residual_specs(tm, d, geo["midx"](tm), 2)
    return pl.pallas_call(
        _down_kernel,
        grid=(t // tm, d // tn),
        in_specs=[pl.BlockSpec((tm, kf), lambda i, j: (i, 0)), pl.BlockSpec((kf, tn), lambda i, j: (0, j))] + r_ins,
        out_specs=r_outs,
        out_shape=[jax.ShapeDtypeStruct((t, d), F32), jax.ShapeDtypeStruct((t, d), BF16)],
        scratch_shapes=[pltpu.VMEM((d // tn, tm, tn), F32)],
        compiler_params=_cparams("parallel", "arbitrary"),
        name="ffn_down_residual",
    )(act, w, x, gpost.reshape(1, d), gate, gpre.reshape(1, d), sh, sc)


HALO = 16


def _ffn_up_kernel(hp_ref, h_ref, hn_ref, wa_ref, wv_ref, cwa_ref, cwv_ref, cba_ref, cbv_ref, o_ref,
                   *, n_lat_blk, s, l):
    i = pl.program_id(0)
    tm = h_ref.shape[0]
    hext = jnp.concatenate([hp_ref[...], h_ref[...], hn_ref[...]], axis=0)
    row = lax.broadcasted_iota(jnp.int32, (tm, 1), 0)

    def edges(blk, seq):
        if seq >= tm:
            per = seq // tm
            return tm - 1, blk % per == 0, blk % per == per - 1
        assert seq & (seq - 1) == 0, "short sequences must have power-of-two length"
        return seq - 1, True, True

    is_lat = i < n_lat_blk
    pm_l, bf_l, bl_l = edges(i, s)
    pm_c, bf_c, bl_c = edges(i - n_lat_blk, l)
    pm = jnp.where(is_lat, pm_l, pm_c)
    is_first = jnp.logical_and((row & pm) == 0, jnp.where(is_lat, bf_l, bf_c))
    is_last = jnp.logical_and((row & pm) == pm, jnp.where(is_lat, bl_l, bl_c))

    def conv(w_ref, cw_ref, cb_ref):
        u = _dot(hext, w_ref[...])
        prev = jnp.where(is_first, 0.0, u[HALO - 1:HALO - 1 + tm])
        nxt = jnp.where(is_last, 0.0, u[HALO + 1:HALO + 1 + tm])
        cw = cw_ref[...]
        return prev * cw[0:1] + u[HALO:HALO + tm] * cw[1:2] + nxt * cw[2:3] + cb_ref[...]

    a = conv(wa_ref, cwa_ref, cba_ref)
    v = conv(wv_ref, cwv_ref, cbv_ref)
    o_ref[...] = (_silu(a) * v).astype(o_ref.dtype)


def _ffn_up(h, w_up, conv_w, conv_b, geo, rows):
    t, d = rows, h.shape[1]
    tm = geo["tm_mm"]
    tn = _pick((512, 256, 128), D_FF)
    nj = D_FF // tn
    n_halo = t // HALO
    per = tm // HALO
    cb = conv_b.reshape(1, 2 * D_FF)
    kern = functools.partial(_ffn_up_kernel, n_lat_blk=geo["B"] * geo["S"] // tm, s=geo["S"], l=geo["L"])
    return pl.pallas_call(
        kern,
        grid=(t // tm, nj),
        in_specs=[pl.BlockSpec((HALO, d), lambda i, j: (jnp.maximum(i * per - 1, 0), 0)),
                  pl.BlockSpec((tm, d), lambda i, j: (i, 0)),
                  pl.BlockSpec((HALO, d), lambda i, j: (jnp.minimum((i + 1) * per, n_halo - 1), 0)),
                  pl.BlockSpec((d, tn), lambda i, j: (0, j)),
                  pl.BlockSpec((d, tn), lambda i, j: (0, nj + j)),
                  pl.BlockSpec((3, tn), lambda i, j: (0, j)),
                  pl.BlockSpec((3, tn), lambda i, j: (0, nj + j)),
                  pl.BlockSpec((1, tn), lambda i, j: (0, j)),
                  pl.BlockSpec((1, tn), lambda i, j: (0, nj + j))],
        out_specs=pl.BlockSpec((tm, tn), lambda i, j: (i, j)),
        out_shape=jax.ShapeDtypeStruct((t, D_FF), BF16),
        compiler_params=_cparams("parallel", "arbitrary"),
        name="ffn_up_conv",
    )(h, h, h, w_up, w_up, conv_w, conv_w, cb, cb)


def _rope_tables(s, head_dim, tm):
    quarter = head_dim // 4
    pos = np.arange(s)
    inv = np.power(np.float32(ROPE_BASE), -np.arange(quarter, dtype=np.float32) / quarter).astype(np.float32)
    row = (pos // GRID_W).astype(np.float32)[:, None] * inv
    col = (pos % GRID_W).astype(np.float32)[:, None] * inv
    ang = jnp.asarray(np.concatenate([row, row, col, col], axis=1))
    sign = np.concatenate([-np.ones(quarter), np.ones(quarter)] * 2).astype(np.float32)
    cos = jnp.cos(ang)
    sin = jnp.sin(ang) * sign
    reps = LANES // head_dim
    cos = jnp.tile(cos, (1, reps))
    sin = jnp.tile(sin, (1, reps))
    cos = jnp.concatenate([cos, jnp.ones((tm, LANES), F32)], axis=0)
    sin = jnp.concatenate([sin, jnp.zeros((tm, LANES), F32)], axis=0)
    return cos, sin, quarter


def _geometry(bsz, s, l):
    nl, nc = bsz * s, bsz * l

    def midx(tm):
        nlat, per = nl // tm, s // tm
        return lambda i: jnp.where(i < nlat, 1 + i // per, 0)

    def tabidx(tm):
        nlat, per = nl // tm, s // tm
        return lambda i: jnp.where(i < nlat, i % per, per)

    return {"B": bsz, "S": s, "L": l,
            "tm_mm": _pick((1024, 512, 256, 128), s, nc),
            "tm_row": _pick((512, 256, 128), s, nc),
            "tq_b": _pick((128, 64), s // 2),
            "hgrn_rows": _pick((256, 128, 64), s, l),
            "win_rows": _pick((512, 256, 128), s),
            "midx": midx, "tabidx": tabidx}


def kernel(x, c, ctx, c_ctx, w_ada, b_ada, g_pre_mix, g_post_mix, g_pre_ffn, g_post_ffn, w_in, a_lb_logits, a_norm_g, b_lambda, b_subln_g, c_sink, w_branch, w_out, w_up, conv_w, conv_b, w_down):
    bsz, s, d = x.shape
    l = ctx.shape[1]
    depth = w_in.shape[0]
    assert d == D_MODEL and s % GRID_W == 0 and s % C_BLOCK == 0 and bsz + 1 <= MOD_ROWS
    geo = _geometry(bsz, s, l)
    nl = bsz * s

    xa = jnp.concatenate([x.reshape(nl, d), ctx.reshape(bsz * l, d)], axis=0)

    cc = jnp.concatenate([c_ctx[None], c, jnp.zeros((MOD_ROWS - 1 - bsz, d), F32)], axis=0)
    mod = _modulation(cc, w_ada, b_ada).reshape(depth, MOD_ROWS, 6, 1, d)

    def modv(layer, k):
        return mod[layer, :, k]

    p = jax.nn.softmax(a_lb_logits.astype(F32), axis=1)
    lower = jnp.cumsum(p, axis=1) - p[:, :1]
    hconsts = _hgrn_constants(geo["hgrn_rows"])

    tm = geo["tm_mm"]
    cos_b, sin_b, grp_b = _rope_tables(s, B_DH, tm)
    cos_c, sin_c, grp_c = _rope_tables(s, C_DH, tm)
    scale_b = jnp.concatenate([jnp.full((BRANCH_W,), B_DH ** -0.5 * LOG2E, F32),
                               jnp.ones((BRANCH_W,), F32)]).reshape(1, -1)
    scale_c = jnp.concatenate([jnp.full((BRANCH_W,), C_DH ** -0.5 * LOG2E, F32),
                               jnp.ones((C_KV_HEADS * C_DH,), F32)]).reshape(1, -1)

    e = np.cumsum((0, 1024, 1024, 1024, 1024, 1024, 1024, 1024, 1024, 1024, 256, 256, 6144))
    h = _prenorm(xa, g_pre_mix[0], modv(0, 0), modv(0, 1), geo)
    for layer in range(depth):
        wi = w_in[layer]
        w_nr = jnp.concatenate([wi[:, e[0]:e[1]], wi[:, e[3]:e[5]], wi[:, e[7]:e[8]], wi[:, e[10]:e[11]]],
                               axis=1).astype(BF16)
        w_f = wi[:, e[1]:e[3]].astype(BF16)
        w_rb = wi[:, e[5]:e[7]].astype(BF16)
        w_rc = wi[:, e[8]:e[10]].astype(BF16)
        w_g = wi[:, e[11]:e[12]].astype(BF16)
        lambda_init = 0.8 - 0.6 * math.exp(-0.3 * layer)
        lv = b_lambda[layer].astype(F32)
        lam = jnp.exp(jnp.sum(lv[0] * lv[1])) - jnp.exp(jnp.sum(lv[2] * lv[3])) + lambda_init

        z_nr = _project(h, w_nr, BF16, geo, tn_cands=(2176, 256, 128))
        z_f = _project(h, w_f, F32, geo)
        z_rb = _project(h, w_rb, BF16, geo, epi="rope", rope=(cos_b, sin_b, scale_b, grp_b))
        z_rc = _project(h, w_rc, BF16, geo, epi="rope", rope=(cos_c, sin_c, scale_c, grp_c),
                        tn_cands=(1280, 256, 128))
        gates = _project(h, w_g, BF16, geo, epi="sigmoid")

        oa = _hgrn(z_nr, z_f, lower[:, layer], a_norm_g[layer], hconsts, geo)
        ob = _diff_attn(z_rb, z_nr, lam, b_subln_g[layer], lambda_init, geo)
        oc = _win_attn(z_rc, z_nr, c_sink[layer].astype(F32), geo)
        need_ctx = layer < depth - 1
        rows = xa.shape[0] if need_ctx else nl
        if need_ctx:
            ob = _diff_attn(z_rb, z_nr, lam, b_subln_g[layer], lambda_init, geo, prev=ob)
            oc = _win_attn(z_rc, z_nr, c_sink[layer].astype(F32), geo, prev=oc)

        m = _merge(oa, ob, oc, gates, w_branch[layer].astype(BF16), geo, rows)
        xa, h = _outproj(m, w_out[layer].astype(BF16), xa, g_post_mix[layer], modv(layer, 2),
                         g_pre_ffn[layer], modv(layer, 3), modv(layer, 4), geo, rows)
        act = _ffn_up(h, w_up[layer].astype(BF16), conv_w[layer], conv_b[layer], geo, rows)
        nxt = min(layer + 1, depth - 1)
        xa, h = _ffn_down(act, w_down[layer].astype(BF16), xa, g_post_ffn[layer], modv(layer, 5),
                          g_pre_mix[nxt], modv(nxt, 0), modv(nxt, 1), geo, rows)
    return xa.reshape(bsz, s, d)
```

```python
import functools
import math

import numpy as np
import jax
import jax.numpy as jnp
from jax import lax
from jax.experimental import pallas as pl
from jax.experimental.pallas import tpu as pltpu

F32 = jnp.float32
BF16 = jnp.bfloat16

D_MODEL = 2048
GRID_W = 64
EPS = 1e-6
ROPE_BASE = 10000.0
NEG_BIG = -1e30
BRANCH_W = D_MODEL // 2
A_HEADS = 8
A_DK = 128
A_DV = BRANCH_W // A_HEADS
B_HEADS = 8
B_DH = BRANCH_W // (2 * B_HEADS)
C_Q_HEADS = 8
C_KV_HEADS = 2
C_GROUP = C_Q_HEADS // C_KV_HEADS
C_DH = BRANCH_W // C_Q_HEADS
C_BLOCK = 128
D_FF = 5632
N_BRANCH = 3
LOG2E = 1.4426950408889634

LANES = 128
VMEM_LIMIT_BYTES = 56 * 1024 * 1024
MOD_ROWS = 16
HGRN_HEADS_PER_STEP = 4
ONES_ROWS = 16
SOFTMAX_COLS = 512


def _cparams(*sem):
    return pltpu.CompilerParams(dimension_semantics=sem, vmem_limit_bytes=VMEM_LIMIT_BYTES)


def _pick(cands, *dims):
    for c in cands:
        if all(d % c == 0 for d in dims):
            return c
    raise ValueError(f"no tile in {cands} divides {dims}")


def _dot(a, b):
    return jnp.dot(a, b, preferred_element_type=F32)


def _dot_nt(a, b):
    return lax.dot_general(a, b, (((1,), (1,)), ((), ())), preferred_element_type=F32)


def _dot_tn(a, b):
    return lax.dot_general(a, b, (((0,), (0,)), ((), ())), preferred_element_type=F32)


def _rms(x):
    return lax.rsqrt(jnp.mean(x * x, axis=-1, keepdims=True) + EPS)


def _silu(x):
    return x * jax.nn.sigmoid(x)


def _mod_kernel(c_ref, w_ref, b_ref, o_ref):
    s = _silu(c_ref[...])
    o_ref[0] = _dot(s.astype(BF16), w_ref[0].astype(BF16)) + b_ref[0]


def _modulation(cc, w_ada, b_ada):
    depth, d, n = w_ada.shape
    tn = _pick((1024, 512, 256, 128), n)
    return pl.pallas_call(
        _mod_kernel,
        grid=(depth, n // tn),
        in_specs=[pl.BlockSpec((MOD_ROWS, d), lambda l, j: (0, 0)),
                  pl.BlockSpec((1, d, tn), lambda l, j: (l, 0, j)),
                  pl.BlockSpec((1, 1, tn), lambda l, j: (l, 0, j))],
        out_specs=pl.BlockSpec((1, MOD_ROWS, tn), lambda l, j: (l, 0, j)),
        out_shape=jax.ShapeDtypeStruct((depth, MOD_ROWS, n), F32),
        compiler_params=_cparams("parallel", "parallel"),
        name="modulation",
    )(cc, w_ada, b_ada.reshape(depth, 1, n))


def _prenorm_kernel(x_ref, g_ref, sh_ref, sc_ref, h_ref):
    x = x_ref[...]
    h = (x * _rms(x) * g_ref[...]) * (1.0 + sc_ref[0]) + sh_ref[0]
    h_ref[...] = h.astype(h_ref.dtype)


def _prenorm(x, g, sh, sc, geo):
    t, d = x.shape
    tm = geo["tm_row"]
    midx = geo["midx"](tm)
    return pl.pallas_call(
        _prenorm_kernel,
        grid=(t // tm,),
        in_specs=[pl.BlockSpec((tm, d), lambda i: (i, 0)),
                  pl.BlockSpec((1, d), lambda i: (0, 0)),
                  pl.BlockSpec((1, 1, d), lambda i: (midx(i), 0, 0)),
                  pl.BlockSpec((1, 1, d), lambda i: (midx(i), 0, 0))],
        out_specs=pl.BlockSpec((tm, d), lambda i: (i, 0)),
        out_shape=jax.ShapeDtypeStruct((t, d), BF16),
        compiler_params=_cparams("parallel"),
        name="prenorm",
    )(x, g.reshape(1, d), sh, sc)


def _proj_kernel(h_ref, w_ref, *rest, epi, group):
    acc = _dot(h_ref[...], w_ref[...])
    if epi == "none":
        (o_ref,) = rest
        o_ref[...] = acc.astype(o_ref.dtype)
    elif epi == "sigmoid":
        (o_ref,) = rest
        o_ref[...] = jax.nn.sigmoid(acc).astype(o_ref.dtype)
    else:
        cos_ref, sin_ref, cs_ref, o_ref = rest
        tn = acc.shape[1]
        reps = tn // LANES
        cos = jnp.concatenate([cos_ref[...]] * reps, axis=1)
        sin = jnp.concatenate([sin_ref[...]] * reps, axis=1)
        lane = lax.broadcasted_iota(jnp.int32, acc.shape, 1)
        first = (lane % (2 * group)) < group
        partner = jnp.where(first, pltpu.roll(acc, tn - group, 1), pltpu.roll(acc, group, 1))
        o_ref[...] = ((acc * cos + partner * sin) * cs_ref[...]).astype(o_ref.dtype)


def _project(h, w, out_dtype, geo, epi="none", rope=None, tn_cands=(1024, 512, 256, 128)):
    t, d = h.shape
    n = w.shape[1]
    tm = geo["tm_mm"]
    tn = _pick(tn_cands, n)
    in_specs = [pl.BlockSpec((tm, d), lambda i, j: (i, 0)),
                pl.BlockSpec((d, tn), lambda i, j: (0, j))]
    args = [h, w]
    group = 0
    if epi == "rope":
        cos, sin, colscale, group = rope
        tab = geo["tabidx"](tm)
        in_specs += [pl.BlockSpec((tm, LANES), lambda i, j: (tab(i), 0)),
                     pl.BlockSpec((tm, LANES), lambda i, j: (tab(i), 0)),
                     pl.BlockSpec((1, tn), lambda i, j: (0, j))]
        args += [cos, sin, colscale]
    return pl.pallas_call(
        functools.partial(_proj_kernel, epi=epi, group=group),
        grid=(t // tm, n // tn),
        in_specs=in_specs,
        out_specs=pl.BlockSpec((tm, tn), lambda i, j: (i, j)),
        out_shape=jax.ShapeDtypeStruct((t, n), out_dtype),
        compiler_params=_cparams("parallel", "arbitrary"),
        name=f"proj_{epi}",
    )(*args)


def _hgrn_constants(c):
    nlev = int(math.log2(c))
    assert 1 << nlev == c
    t = np.arange(c)[:, None]
    u = np.arange(c)[None, :]
    w = np.zeros((2, (2 + nlev) * c + ONES_ROWS, c), np.float32)
    lev = np.full((2, c, c), -1, np.int32)
    for d in range(2):
        if d == 0:
            cum, rem = u <= t, u > t
        else:
            cum, rem = u >= t, u < t
        w[d, 0:c] = cum
        w[d, c:2 * c] = rem
        for l in range(nlev):
            h = 1 << l
            base = (t // (2 * h)) * (2 * h)
            if d == 0:
                m = base + h - 1
                e = np.where(t > m, (u > m) & (u <= t), (u > t) & (u <= m))
                pair = (t // (2 * h) == u // (2 * h)) & (t % (2 * h) >= h) & (u % (2 * h) < h)
            else:
                m = base + h
                e = np.where(t < m, (u >= t) & (u < m), (u >= m) & (u < t))
                pair = (t // (2 * h) == u // (2 * h)) & (t % (2 * h) < h) & (u % (2 * h) >= h)
            w[d, (2 + l) * c:(3 + l) * c] = e
            lev[d][pair] = l
        w[d, (2 + nlev) * c:] = 1.0
        lev[d][np.arange(c), np.arange(c)] = nlev
    return jnp.asarray(w, BF16), jnp.asarray(lev), nlev


def _hgrn_kernel(zq_ref, zf_ref, zi_ref, zg_ref, lb_ref, ng_ref, w_ref, lev_ref, o_ref,
                 st_ref, ofwd_ref, *, nlev, n_ctx_blk, n_lat_blk):
    c = lev_ref.shape[1]
    assert zq_ref.shape[0] == c
    half = c // 2
    d = pl.program_id(2)
    j = pl.program_id(3)

    @pl.when(j == 0)
    def _():
        st_ref[...] = jnp.zeros_like(st_ref)

    slot = jnp.where(d == 0, j,
                     jnp.where(j < n_ctx_blk, n_ctx_blk - 1 - j,
                               2 * n_ctx_blk + n_lat_blk - 1 - j))
    srow = pl.ds(pl.multiple_of(slot * c, c), c)
    halves = (slice(0, half), slice(half, c))

    def run(direction):
        lb = lb_ref[0]
        lev = lev_ref[0]
        low_levels = list(range(nlev - 1))
        masks = [{l: lev[r, r] == l for l in low_levels + [nlev]} for r in halves]
        zf = zf_ref[...]
        sig = jax.nn.sigmoid(zf)
        f = lb + (1.0 - lb) * sig
        kk = (1.0 - lb) * (1.0 - sig)
        g = jnp.log(f)
        a = jnp.exp(_dot(w_ref[0], g.astype(BF16)))
        q = _silu(zq_ref[...].astype(F32))
        v = zi_ref[...]
        q_top, k_top = (halves[1], halves[0]) if direction == 0 else (halves[0], halves[1])
        for hh in range(HGRN_HEADS_PER_STEP):
            ls = slice(hh * LANES, (hh + 1) * LANES)
            qh, kh, vh = q[:, ls], kk[:, ls], v[:, ls]
            diag = []
            for r, mk in zip(halves, masks):
                qr, kr = qh[r], kh[r]
                sc = jnp.where(mk[nlev], _dot_nt(qr.astype(BF16), kr.astype(BF16)), 0.0)
                for l in low_levels:
                    al = a[(2 + l) * c:(3 + l) * c, ls][r]
                    sc = jnp.where(mk[l], _dot_nt((qr * al).astype(BF16), (kr * al).astype(BF16)), sc)
                diag.append(sc.astype(BF16))
            at = a[(1 + nlev) * c:(2 + nlev) * c, ls]
            off = _dot_nt((qh[q_top] * at[q_top]).astype(BF16), (kh[k_top] * at[k_top]).astype(BF16))
            st = st_ref[hh]
            stb = st.astype(BF16)
            qs = (qh * a[0:c, ls]).astype(BF16)
            outs = [_dot(diag[i], vh[r]) + _dot_nt(qs[r], stb) for i, r in enumerate(halves)]
            i_top = 1 if direction == 0 else 0
            outs[i_top] = outs[i_top] + _dot(off.astype(BF16), vh[k_top])
            ut = _dot_tn(vh, (kh * a[c:2 * c, ls]).astype(BF16))
            dec = a[(2 + nlev) * c:(2 + nlev) * c + 1, ls]
            st_ref[hh] = st * dec + ut
            o = jnp.concatenate(outs, axis=0)
            if direction == 0:
                ofwd_ref[srow, ls] = o
            else:
                tot = ofwd_ref[srow, ls] + o
                gate = _silu(zg_ref[:, ls].astype(F32))
                o_ref[:, ls] = (tot * _rms(tot) * ng_ref[:, ls] * gate).astype(o_ref.dtype)

    @pl.when(d == 0)
    def _():
        run(0)

    @pl.when(d == 1)
    def _():
        run(1)


def _hgrn(z_nr, z_f, lb, norm_g, consts, geo):
    w_c, lev_c, nlev = consts
    bsz, s, l = geo["B"], geo["S"], geo["L"]
    t = z_nr.shape[0]
    r = geo["hgrn_rows"]
    nc, nl = l // r, s // r
    wcols = HGRN_HEADS_PER_STEP * LANES
    hp = BRANCH_W // wcols
    ctx0 = bsz * s // r

    def rowblk(b, d, j):
        slot = jnp.where(d == 0, j, jnp.where(j < nc, nc - 1 - j, 2 * nc + nl - 1 - j))
        return jnp.where(slot < nc, ctx0 + b * nc + slot, b * nl + slot - nc)

    def zmap(colblk0):
        return lambda b, p, d, j: (rowblk(b, d, j), colblk0 + p)

    def outmap(b, p, d, j):
        return (rowblk(b, 1, jnp.where(d == 0, 0, j)), p)

    kern = functools.partial(_hgrn_kernel, nlev=nlev, n_ctx_blk=nc, n_lat_blk=nl)
    return pl.pallas_call(
        kern,
        grid=(bsz, hp, 2, nc + nl),
        in_specs=[pl.BlockSpec((r, wcols), zmap(0)),
                  pl.BlockSpec((r, wcols), lambda b, p, d, j: (rowblk(b, d, j), d * hp + p)),
                  pl.BlockSpec((r, wcols), zmap(hp)),
                  pl.BlockSpec((r, wcols), zmap(2 * hp)),
                  pl.BlockSpec((1, 1, wcols), lambda b, p, d, j: (d, 0, p)),
                  pl.BlockSpec((1, wcols), lambda b, p, d, j: (0, p)),
                  pl.BlockSpec((1,) + w_c.shape[1:], lambda b, p, d, j: (d, 0, 0)),
                  pl.BlockSpec((1,) + lev_c.shape[1:], lambda b, p, d, j: (d, 0, 0))],
        out_specs=pl.BlockSpec((r, wcols), outmap),
        out_shape=jax.ShapeDtypeStruct((t, BRANCH_W), BF16),
        scratch_shapes=[pltpu.VMEM((HGRN_HEADS_PER_STEP, A_DV, A_DK), F32),
                        pltpu.VMEM((s + l, wcols), F32)],
        compiler_params=_cparams("parallel", "parallel", "arbitrary", "arbitrary"),
        name="hgrn2",
    )(z_nr, z_f, z_nr, z_nr, lb.reshape(2, 1, BRANCH_W),
      jnp.tile(norm_g, A_HEADS).reshape(1, BRANCH_W), w_c, lev_c)


def _diff_attn_kernel(lam_ref, q_ref, *rest, n_seg, out_scale):
    kv = rest[:2 * n_seg]
    g_ref, o_ref = rest[2 * n_seg], rest[2 * n_seg + 1]
    q = q_ref[...]
    lane = lax.broadcasted_iota(jnp.int32, q.shape, 1)
    zero = jnp.zeros_like(q)
    qs = (jnp.where(lane < B_DH, q, zero), jnp.where(lane >= B_DH, q, zero))
    es, inv = [], []
    for comp in range(2):
        s = [_dot_nt(qs[comp], kv[2 * i][...]) for i in range(n_seg)]
        m = s[0].max(axis=-1, keepdims=True)
        for x in s[1:]:
            m = jnp.maximum(m, x.max(axis=-1, keepdims=True))
        e = [jnp.exp2(x - m) for x in s]
        tot = e[0].sum(axis=-1, keepdims=True)
        for x in e[1:]:
            tot = tot + x.sum(axis=-1, keepdims=True)
        es.append(e)
        inv.append(1.0 / tot)
    r0 = inv[0]
    r1 = inv[1] * lam_ref[0, 0]
    o = None
    for i in range(n_seg):
        wgt = (es[0][i] * r0 - es[1][i] * r1).astype(BF16)
        part = _dot(wgt, kv[2 * i + 1][...])
        o = part if o is None else o + part
    o_ref[...] = (o * _rms(o) * (g_ref[...] * out_scale)).astype(o_ref.dtype)


def _diff_attn_pipe_kernel(lam_ref, q_ref, kl_ref, vl_ref, kc_ref, vc_ref, g_ref, o_ref,
                           s0_ref, s1_ref, m0_ref, m1_ref, e0_ref, e1_ref, wa_ref, wb_ref, ra_ref, rb_ref, *, tq, out_scale):
    s_len = kl_ref.shape[0]
    n = q_ref.shape[0] // tq
    lam = lam_ref[0, 0]
    gain = g_ref[...] * out_scale
    sa_ref, sb_ref = (s0_ref, m0_ref, e0_ref), (s1_ref, m1_ref, e1_ref)

    def rows(j):
        return pl.ds(pl.multiple_of(j * tq, tq), tq)

    def scores(j, sm_refs):
        s_ref, m_ref, _ = sm_refs
        q = q_ref[rows(j), :]
        lane = lax.broadcasted_iota(jnp.int32, q.shape, 1)
        zero = jnp.zeros_like(q)
        for comp, qc in enumerate((jnp.where(lane < B_DH, q, zero), jnp.where(lane >= B_DH, q, zero))):
            s_lat = _dot_nt(qc, kl_ref[...])
            s_ctx = _dot_nt(qc, kc_ref[...])
            s_ref[comp, :, 0:s_len] = s_lat
            s_ref[comp, :, s_len:] = s_ctx
            m_ref[comp] = jnp.maximum(s_lat.max(axis=-1, keepdims=True), s_ctx.max(axis=-1, keepdims=True))

    nk = s_len + kc_ref.shape[0]
    col_chunks = [(c0, min(c0 + SOFTMAX_COLS, nk)) for c0 in range(0, nk, SOFTMAX_COLS)]

    def softmax(sm_refs, w_ref, r_ref):
        s_ref, m_ref, e_ref = sm_refs
        ls = []
        for comp in range(2):
            m = m_ref[comp]
            acc = jnp.zeros((tq, LANES), F32)
            for c0, c1 in col_chunks:
                e = jnp.exp2(s_ref[comp, :, c0:c1] - m)
                for k in range(0, c1 - c0, LANES):
                    acc = acc + e[:, k:k + LANES]
                e_ref[comp, :, c0:c1] = e.astype(BF16)
            ls.append(acc.sum(axis=-1, keepdims=True))
        coef = (lam * ls[0] / ls[1]).astype(BF16)
        for c0, c1 in col_chunks:
            w_ref[:, c0:c1] = e_ref[0, :, c0:c1] - coef * e_ref[1, :, c0:c1]
        r_ref[...] = 1.0 / ls[0]

    def values(j, w_ref, r_ref):
        o = (_dot(w_ref[:, 0:s_len], vl_ref[...]) + _dot(w_ref[:, s_len:], vc_ref[...])) * r_ref[...]
        o_ref[rows(j), :] = (o * _rms(o) * gain).astype(o_ref.dtype)

    scores(0, sa_ref)
    scores(1, sb_ref)
    softmax(sa_ref, wa_ref, ra_ref)

    def pair(u, carry):
        t = 2 * u
        values(t - 2, wa_ref, ra_ref)
        softmax(sb_ref, wb_ref, rb_ref)
        scores(t, sa_ref)
        values(t - 1, wb_ref, rb_ref)
        softmax(sa_ref, wa_ref, ra_ref)
        scores(t + 1, sb_ref)
        return carry

    lax.fori_loop(1, n // 2, pair, 0)
    softmax(sb_ref, wb_ref, rb_ref)
    values(n - 2, wa_ref, ra_ref)
    values(n - 1, wb_ref, rb_ref)


def _diff_attn_lat(z_rb, z_nr, lam, subln_g, lambda_init, geo):
    bsz, s, l = geo["B"], geo["S"], geo["L"]
    t = z_rb.shape[0]
    ctx0 = bsz * s // l
    kcol, vcol = B_HEADS, 3 * B_HEADS
    tq = geo["tq_b"]
    assert (s // tq) % 2 == 0
    nk = s + l
    return pl.pallas_call(
        functools.partial(_diff_attn_pipe_kernel, tq=tq, out_scale=1.0 - lambda_init),
        grid=(bsz, B_HEADS),
        in_specs=[pl.BlockSpec(memory_space=pltpu.SMEM),
                  pl.BlockSpec((s, LANES), lambda b, h: (b, h)),
                  pl.BlockSpec((s, LANES), lambda b, h: (b, kcol + h)),
                  pl.BlockSpec((s, LANES), lambda b, h: (b, vcol + h)),
                  pl.BlockSpec((l, LANES), lambda b, h: (ctx0 + b, kcol + h)),
                  pl.BlockSpec((l, LANES), lambda b, h: (ctx0 + b, vcol + h)),
                  pl.BlockSpec((1, LANES), lambda b, h: (0, 0))],
        out_specs=pl.BlockSpec((s, LANES), lambda b, h: (b, h)),
        out_shape=jax.ShapeDtypeStruct((t, BRANCH_W), BF16),
        scratch_shapes=[pltpu.VMEM((2, tq, nk), F32), pltpu.VMEM((2, tq, nk), F32),
                        pltpu.VMEM((2, tq, 1), F32), pltpu.VMEM((2, tq, 1), F32),
                        pltpu.VMEM((2, tq, nk), BF16), pltpu.VMEM((2, tq, nk), BF16),
                        pltpu.VMEM((tq, nk), BF16), pltpu.VMEM((tq, nk), BF16),
                        pltpu.VMEM((tq, 1), F32), pltpu.VMEM((tq, 1), F32)],
        compiler_params=_cparams("parallel", "parallel"),
        name="diff_attn_lat",
    )(lam.reshape(1, 1), z_rb, z_rb, z_nr, z_rb, z_nr, subln_g.reshape(1, 2 * B_DH))


def _diff_attn(z_rb, z_nr, lam, subln_g, lambda_init, geo, prev=None):
    bsz, s, l = geo["B"], geo["S"], geo["L"]
    t = z_rb.shape[0]
    ctx0 = bsz * s // l
    kcol, vcol = B_HEADS, 3 * B_HEADS
    g2 = subln_g.reshape(1, 2 * B_DH)
    lam2 = lam.reshape(1, 1)
    if prev is None:
        return _diff_attn_lat(z_rb, z_nr, lam, subln_g, lambda_init, geo)

    def kern(lam_ref, q_ref, k_ref, v_ref, g_ref, prev_ref, o_ref):
        del prev_ref
        _diff_attn_kernel(lam_ref, q_ref, k_ref, v_ref, g_ref, o_ref, n_seg=1, out_scale=1.0 - lambda_init)

    return pl.pallas_call(
        kern,
        grid=(bsz, B_HEADS),
        in_specs=[pl.BlockSpec(memory_space=pltpu.SMEM),
                  pl.BlockSpec((l, LANES), lambda b, h: (ctx0 + b, h)),
                  pl.BlockSpec((l, LANES), lambda b, h: (ctx0 + b, kcol + h)),
                  pl.BlockSpec((l, LANES), lambda b, h: (ctx0 + b, vcol + h)),
                  pl.BlockSpec((1, LANES), lambda b, h: (0, 0)),
                  pl.BlockSpec(memory_space=pl.ANY)],
        out_specs=pl.BlockSpec((l, LANES), lambda b, h: (ctx0 + b, h)),
        out_shape=jax.ShapeDtypeStruct((t, BRANCH_W), BF16),
        input_output_aliases={5: 0},
        compiler_params=_cparams("parallel", "parallel"),
        name="diff_attn_ctx",
    )(lam2, z_rb, z_rb, z_nr, g2, prev)


def _stack_heads(q_ref, r0, rows):
    return jnp.concatenate([q_ref[r0:r0 + rows, r * LANES:(r + 1) * LANES] for r in range(C_GROUP)], axis=0)


def _sink_column(sink_ref, g, rows_per_head):
    rid = lax.broadcasted_iota(jnp.int32, (C_GROUP * rows_per_head, 1), 0) // rows_per_head
    col = jnp.full(rid.shape, sink_ref[g * C_GROUP] * LOG2E, F32)
    for r in range(1, C_GROUP):
        col = jnp.where(rid == r, sink_ref[g * C_GROUP + r] * LOG2E, col)
    return col


def _lane_groups(x):
    return [x[:, k:k + LANES] for k in range(0, x.shape[1], LANES)]


def _sink_attend(scores, values, sink_col, o_ref, r0, rows_per_head):
    groups = [grp for x in scores for grp in _lane_groups(x)]
    mvec = groups[0]
    for grp in groups[1:]:
        mvec = jnp.maximum(mvec, grp)
    m = jnp.maximum(sink_col, mvec.max(axis=-1, keepdims=True))
    e = [jnp.exp2(x - m) for x in scores]
    egroups = [grp for x in e for grp in _lane_groups(x)]
    svec = egroups[0]
    for grp in egroups[1:]:
        svec = svec + grp
    inv = 1.0 / (jnp.exp2(sink_col - m) + svec.sum(axis=-1, keepdims=True))
    o = None
    for x, v in zip(e, values):
        part = _dot(x.astype(BF16), v)
        o = part if o is None else o + part
    o = o * inv
    for r in range(C_GROUP):
        o_ref[r0:r0 + rows_per_head, r * LANES:(r + 1) * LANES] = (
            o[r * rows_per_head:(r + 1) * rows_per_head].astype(o_ref.dtype))


def _win_attn_kernel(sink_ref, q_ref, kc_ref, vc_ref, kp_ref, km_ref, kn_ref, vp_ref, vm_ref, vn_ref,
                     o_ref, *, n_step):
    g = pl.program_id(1)
    n = pl.program_id(2)
    n_sub = q_ref.shape[0] // C_BLOCK
    rows = C_GROUP * C_BLOCK
    tq = lax.broadcasted_iota(jnp.int32, (rows, C_BLOCK), 0) % C_BLOCK
    kj = lax.broadcasted_iota(jnp.int32, (rows, C_BLOCK), 1)
    sink_col = _sink_column(sink_ref, g, C_BLOCK)

    def blk(ref, a):
        return ref[a * C_BLOCK:(a + 1) * C_BLOCK, :]

    kband = [kp_ref[...]] + [blk(km_ref, a) for a in range(n_sub)] + [kn_ref[...]]
    vband = [vp_ref[...]] + [blk(vm_ref, a) for a in range(n_sub)] + [vn_ref[...]]
    for a in range(n_sub):
        q4 = _stack_heads(q_ref, a * C_BLOCK, C_BLOCK)
        has_prev = True if a > 0 else n > 0
        has_next = True if a < n_sub - 1 else n < n_step - 1
        sp = jnp.where(jnp.logical_and(kj >= tq, has_prev), _dot_nt(q4, kband[a]), NEG_BIG)
        sn = jnp.where(jnp.logical_and(kj <= tq, has_next), _dot_nt(q4, kband[a + 2]), NEG_BIG)
        scores = [_dot_nt(q4, kc_ref[...]), sp, _dot_nt(q4, kband[a + 1]), sn]
        values = [vc_ref[...], vband[a], vband[a + 1], vband[a + 2]]
        _sink_attend(scores, values, sink_col, o_ref, a * C_BLOCK, C_BLOCK)


def _win_attn_ctx_kernel(sink_ref, q_ref, kc_ref, vc_ref, prev_ref, o_ref):
    del prev_ref
    g = pl.program_id(1)
    rows_per_head = q_ref.shape[0]
    q4 = _stack_heads(q_ref, 0, rows_per_head)
    _sink_attend([_dot_nt(q4, kc_ref[...])], [vc_ref[...]],
                 _sink_column(sink_ref, g, rows_per_head), o_ref, 0, rows_per_head)


def _win_attn(z_rc, z_nr, sink, geo, prev=None):
    bsz, s, l = geo["B"], geo["S"], geo["L"]
    t = z_rc.shape[0]
    n_blk = s // C_BLOCK
    qw = C_GROUP * LANES
    kcol = C_Q_HEADS
    vcol = (4 * BRANCH_W) // LANES
    ctx0 = bsz * s // l
    smem = pl.BlockSpec(memory_space=pltpu.SMEM)
    if prev is None:
        qb = geo["win_rows"]
        sub = qb // C_BLOCK
        n_step = s // qb

        def edge(colblk0, shift):
            def imap(b, g, n):
                return (b * n_blk + jnp.clip(n * sub + shift, 0, n_blk - 1), colblk0 + g)
            return pl.BlockSpec((C_BLOCK, LANES), imap)

        def mid(colblk0):
            return pl.BlockSpec((qb, LANES), lambda b, g, n: (b * n_step + n, colblk0 + g))

        in_specs = [smem,
                    pl.BlockSpec((qb, qw), lambda b, g, n: (b * n_step + n, g)),
                    pl.BlockSpec((l, LANES), lambda b, g, n: (ctx0 + b, kcol + g)),
                    pl.BlockSpec((l, LANES), lambda b, g, n: (ctx0 + b, vcol + g)),
                    edge(kcol, -1), mid(kcol), edge(kcol, sub),
                    edge(vcol, -1), mid(vcol), edge(vcol, sub)]
        args = [sink, z_rc, z_rc, z_nr, z_rc, z_rc, z_rc, z_nr, z_nr, z_nr]
        return pl.pallas_call(
            functools.partial(_win_attn_kernel, n_step=n_step),
            grid=(bsz, C_KV_HEADS, n_step),
            in_specs=in_specs,
            out_specs=pl.BlockSpec((qb, qw), lambda b, g, n: (b * n_step + n, g)),
            out_shape=jax.ShapeDtypeStruct((t, BRANCH_W), BF16),
            compiler_params=_cparams("parallel", "parallel", "arbitrary"),
            name="win_attn_lat",
        )(*args)
    in_specs = [smem,
                pl.BlockSpec((l, qw), lambda b, g: (ctx0 + b, g)),
                pl.BlockSpec((l, LANES), lambda b, g: (ctx0 + b, kcol + g)),
                pl.BlockSpec((l, LANES), lambda b, g: (ctx0 + b, vcol + g)),
                pl.BlockSpec(memory_space=pl.ANY)]
    return pl.pallas_call(
        _win_attn_ctx_kernel,
        grid=(bsz, C_KV_HEADS),
        in_specs=in_specs,
        out_specs=pl.BlockSpec((l, qw), lambda b, g: (ctx0 + b, g)),
        out_shape=jax.ShapeDtypeStruct((t, BRANCH_W), BF16),
        input_output_aliases={4: 0},
        compiler_params=_cparams("parallel", "parallel"),
        name="win_attn_ctx",
    )(sink, z_rc, z_rc, z_nr, prev)


def _merge_kernel(a_ref, b_ref, c_ref, w_ref, ga_ref, gb_ref, gc_ref, o_ref):
    acc = ga_ref[...].astype(F32) * _dot(a_ref[...], w_ref[0])
    acc = acc + gb_ref[...].astype(F32) * _dot(b_ref[...], w_ref[1])
    acc = acc + gc_ref[...].astype(F32) * _dot(c_ref[...], w_ref[2])
    o_ref[...] = acc.astype(o_ref.dtype)


def _merge(oa, ob, oc, gates, w_branch, geo, rows):
    t = rows
    tm = geo["tm_mm"]
    tn = 512
    nj = D_MODEL // tn
    br = pl.BlockSpec((tm, BRANCH_W), lambda i, j: (i, 0))

    def gate(n):
        return pl.BlockSpec((tm, tn), lambda i, j: (i, n * nj + j))

    return pl.pallas_call(
        _merge_kernel,
        grid=(t // tm, nj),
        in_specs=[br, br, br,
                  pl.BlockSpec((N_BRANCH, BRANCH_W, tn), lambda i, j: (0, 0, j)),
                  gate(0), gate(1), gate(2)],
        out_specs=pl.BlockSpec((tm, tn), lambda i, j: (i, j)),
        out_shape=jax.ShapeDtypeStruct((t, D_MODEL), BF16),
        compiler_params=_cparams("parallel", "arbitrary"),
        name="branch_merge",
    )(oa, ob, oc, w_branch, gates, gates, gates)


def _residual_epilogue(y, x_ref, gpost_ref, gate_ref, gpre_ref, sh_ref, sc_ref, xo_ref, ho_ref):
    x1 = x_ref[...] + gate_ref[0] * (y * _rms(y) * gpost_ref[...])
    xo_ref[...] = x1
    h = (x1 * _rms(x1) * gpre_ref[...]) * (1.0 + sc_ref[0]) + sh_ref[0]
    ho_ref[...] = h.astype(ho_ref.dtype)


def _outproj_kernel(m_ref, w_ref, *rest):
    _residual_epilogue(_dot(m_ref[...], w_ref[...]), *rest)


def _down_kernel(a_ref, w_ref, x_ref, gpost_ref, gate_ref, gpre_ref, sh_ref, sc_ref, xo_ref, ho_ref, y_ref):
    j = pl.program_id(1)
    nj = y_ref.shape[0]
    y_ref[j] = _dot(a_ref[...], w_ref[...])

    @pl.when(j == nj - 1)
    def _():
        y = jnp.concatenate([y_ref[jj] for jj in range(nj)], axis=1)
        _residual_epilogue(y, x_ref, gpost_ref, gate_ref, gpre_ref, sh_ref, sc_ref, xo_ref, ho_ref)


def _residual_specs(tm, d, midx, grid_rank):
    if grid_rank == 1:
        row = lambda i: (i, 0)
        const = lambda i: (0, 0)
        mod = lambda i: (midx(i), 0, 0)
    else:
        row = lambda i, k: (i, 0)
        const = lambda i, k: (0, 0)
        mod = lambda i, k: (midx(i), 0, 0)
    ins = [pl.BlockSpec((tm, d), row), pl.BlockSpec((1, d), const), pl.BlockSpec((1, 1, d), mod),
           pl.BlockSpec((1, d), const), pl.BlockSpec((1, 1, d), mod), pl.BlockSpec((1, 1, d), mod)]
    outs = [pl.BlockSpec((tm, d), row), pl.BlockSpec((tm, d), row)]
    return ins, outs


def _outproj(m, w, x, gpost, gate, gpre, sh, sc, geo, rows):
    t, d = rows, x.shape[1]
    tm = geo["tm_row"]
    r_ins, r_outs = _residual_specs(tm, d, geo["midx"](tm), 1)
    return pl.pallas_call(
        _outproj_kernel,
        grid=(t // tm,),
        in_specs=[pl.BlockSpec((tm, d), lambda i: (i, 0)), pl.BlockSpec((d, d), lambda i: (0, 0))] + r_ins,
        out_specs=r_outs,
        out_shape=[jax.ShapeDtypeStruct((t, d), F32), jax.ShapeDtypeStruct((t, d), BF16)],
        compiler_params=_cparams("parallel"),
        name="outproj_residual",
    )(m, w, x, gpost.reshape(1, d), gate, gpre.reshape(1, d), sh, sc)


def _ffn_down(act, w, x, gpost, gate, gpre, sh, sc, geo, rows):
    t, d = rows, x.shape[1]
    kf = act.shape[1]
    tm = geo["tm_row"]
    tn = _pick((512, 256, 128), d)
    r_ins, r_outs = _residual_specs(tm, d, geo["midx"](tm), 2)
    return pl.pallas_call(
        _down_kernel,
        grid=(t // tm, d // tn),
        in_specs=[pl.BlockSpec((tm, kf), lambda i, j: (i, 0)), pl.BlockSpec((kf, tn), lambda i, j: (0, j))] + r_ins,
        out_specs=r_outs,
        out_shape=[jax.ShapeDtypeStruct((t, d), F32), jax.ShapeDtypeStruct((t, d), BF16)],
        scratch_shapes=[pltpu.VMEM((d // tn, tm, tn), F32)],
        compiler_params=_cparams("parallel", "arbitrary"),
        name="ffn_down_residual",
    )(act, w, x, gpost.reshape(1, d), gate, gpre.reshape(1, d), sh, sc)


HALO = 16


def _ffn_up_kernel(hp_ref, h_ref, hn_ref, wa_ref, wv_ref, cwa_ref, cwv_ref, cba_ref, cbv_ref, o_ref,
                   *, n_lat_blk, s, l):
    i = pl.program_id(0)
    tm = h_ref.shape[0]
    hext = jnp.concatenate([hp_ref[...], h_ref[...], hn_ref[...]], axis=0)
    row = lax.broadcasted_iota(jnp.int32, (tm, 1), 0)

    def edges(blk, seq):
        if seq >= tm:
            per = seq // tm
            return tm - 1, blk % per == 0, blk % per == per - 1
        assert seq & (seq - 1) == 0, "short sequences must have power-of-two length"
        return seq - 1, True, True

    is_lat = i < n_lat_blk
    pm_l, bf_l, bl_l = edges(i, s)
    pm_c, bf_c, bl_c = edges(i - n_lat_blk, l)
    pm = jnp.where(is_lat, pm_l, pm_c)
    is_first = jnp.logical_and((row & pm) == 0, jnp.where(is_lat, bf_l, bf_c))
    is_last = jnp.logical_and((row & pm) == pm, jnp.where(is_lat, bl_l, bl_c))

    def conv(w_ref, cw_ref, cb_ref):
        u = _dot(hext, w_ref[...])
        prev = jnp.where(is_first, 0.0, u[HALO - 1:HALO - 1 + tm])
        nxt = jnp.where(is_last, 0.0, u[HALO + 1:HALO + 1 + tm])
        cw = cw_ref[...]
        return prev * cw[0:1] + u[HALO:HALO + tm] * cw[1:2] + nxt * cw[2:3] + cb_ref[...]

    a = conv(wa_ref, cwa_ref, cba_ref)
    v = conv(wv_ref, cwv_ref, cbv_ref)
    o_ref[...] = (_silu(a) * v).astype(o_ref.dtype)


def _ffn_up(h, w_up, conv_w, conv_b, geo, rows):
    t, d = rows, h.shape[1]
    tm = geo["tm_mm"]
    tn = _pick((512, 256, 128), D_FF)
    nj = D_FF // tn
    n_halo = t // HALO
    per = tm // HALO
    cb = conv_b.reshape(1, 2 * D_FF)
    kern = functools.partial(_ffn_up_kernel, n_lat_blk=geo["B"] * geo["S"] // tm, s=geo["S"], l=geo["L"])
    return pl.pallas_call(
        kern,
        grid=(t // tm, nj),
        in_specs=[pl.BlockSpec((HALO, d), lambda i, j: (jnp.maximum(i * per - 1, 0), 0)),
                  pl.BlockSpec((tm, d), lambda i, j: (i, 0)),
                  pl.BlockSpec((HALO, d), lambda i, j: (jnp.minimum((i + 1) * per, n_halo - 1), 0)),
                  pl.BlockSpec((d, tn), lambda i, j: (0, j)),
                  pl.BlockSpec((d, tn), lambda i, j: (0, nj + j)),
                  pl.BlockSpec((3, tn), lambda i, j: (0, j)),
                  pl.BlockSpec((3, tn), lambda i, j: (0, nj + j)),
                  pl.BlockSpec((1, tn), lambda i, j: (0, j)),
                  pl.BlockSpec((1, tn), lambda i, j: (0, nj + j))],
        out_specs=pl.BlockSpec((tm, tn), lambda i, j: (i, j)),
        out_shape=jax.ShapeDtypeStruct((t, D_FF), BF16),
        compiler_params=_cparams("parallel", "arbitrary"),
        name="ffn_up_conv",
    )(h, h, h, w_up, w_up, conv_w, conv_w, cb, cb)


def _rope_tables(s, head_dim, tm):
    quarter = head_dim // 4
    pos = np.arange(s)
    inv = np.power(np.float32(ROPE_BASE), -np.arange(quarter, dtype=np.float32) / quarter).astype(np.float32)
    row = (pos // GRID_W).astype(np.float32)[:, None] * inv
    col = (pos % GRID_W).astype(np.float32)[:, None] * inv
    ang = jnp.asarray(np.concatenate([row, row, col, col], axis=1))
    sign = np.concatenate([-np.ones(quarter), np.ones(quarter)] * 2).astype(np.float32)
    cos = jnp.cos(ang)
    sin = jnp.sin(ang) * sign
    reps = LANES // head_dim
    cos = jnp.tile(cos, (1, reps))
    sin = jnp.tile(sin, (1, reps))
    cos = jnp.concatenate([cos, jnp.ones((tm, LANES), F32)], axis=0)
    sin = jnp.concatenate([sin, jnp.zeros((tm, LANES), F32)], axis=0)
    return cos, sin, quarter


def _geometry(bsz, s, l):
    nl, nc = bsz * s, bsz * l

    def midx(tm):
        nlat, per = nl // tm, s // tm
        return lambda i: jnp.where(i < nlat, 1 + i // per, 0)

    def tabidx(tm):
        nlat, per = nl // tm, s // tm
        return lambda i: jnp.where(i < nlat, i % per, per)

    return {"B": bsz, "S": s, "L": l,
            "tm_mm": _pick((1024, 512, 256, 128), s, nc),
            "tm_row": _pick((512, 256, 128), s, nc),
            "tq_b": _pick((128, 64), s // 2),
            "hgrn_rows": _pick((256, 128, 64), s, l),
            "win_rows": _pick((512, 256, 128), s),
            "midx": midx, "tabidx": tabidx}


def kernel(x, c, ctx, c_ctx, w_ada, b_ada, g_pre_mix, g_post_mix, g_pre_ffn, g_post_ffn, w_in, a_lb_logits, a_norm_g, b_lambda, b_subln_g, c_sink, w_branch, w_out, w_up, conv_w, conv_b, w_down):
    bsz, s, d = x.shape
    l = ctx.shape[1]
    depth = w_in.shape[0]
    assert d == D_MODEL and s % GRID_W == 0 and s % C_BLOCK == 0 and bsz + 1 <= MOD_ROWS
    geo = _geometry(bsz, s, l)
    nl = bsz * s

    xa = jnp.concatenate([x.reshape(nl, d), ctx.reshape(bsz * l, d)], axis=0)

    cc = jnp.concatenate([c_ctx[None], c, jnp.zeros((MOD_ROWS - 1 - bsz, d), F32)], axis=0)
    mod = _modulation(cc, w_ada, b_ada).reshape(depth, MOD_ROWS, 6, 1, d)

    def modv(layer, k):
        return mod[layer, :, k]

    p = jax.nn.softmax(a_lb_logits.astype(F32), axis=1)
    lower = jnp.cumsum(p, axis=1) - p[:, :1]
    hconsts = _hgrn_constants(geo["hgrn_rows"])

    tm = geo["tm_mm"]
    cos_b, sin_b, grp_b = _rope_tables(s, B_DH, tm)
    cos_c, sin_c, grp_c = _rope_tables(s, C_DH, tm)
    scale_b = jnp.concatenate([jnp.full((BRANCH_W,), B_DH ** -0.5 * LOG2E, F32),
                               jnp.ones((BRANCH_W,), F32)]).reshape(1, -1)
    scale_c = jnp.concatenate([jnp.full((BRANCH_W,), C_DH ** -0.5 * LOG2E, F32),
                               jnp.ones((C_KV_HEADS * C_DH,), F32)]).reshape(1, -1)

    e = np.cumsum((0, 1024, 1024, 1024, 1024, 1024, 1024, 1024, 1024, 1024, 256, 256, 6144))
    h = _prenorm(xa, g_pre_mix[0], modv(0, 0), modv(0, 1), geo)
    for layer in range(depth):
        wi = w_in[layer]
        w_nr = jnp.concatenate([wi[:, e[0]:e[1]], wi[:, e[3]:e[5]], wi[:, e[7]:e[8]], wi[:, e[10]:e[11]]],
                               axis=1).astype(BF16)
        w_f = wi[:, e[1]:e[3]].astype(BF16)
        w_rb = wi[:, e[5]:e[7]].astype(BF16)
        w_rc = wi[:, e[8]:e[10]].astype(BF16)
        w_g = wi[:, e[11]:e[12]].astype(BF16)
        lambda_init = 0.8 - 0.6 * math.exp(-0.3 * layer)
        lv = b_lambda[layer].astype(F32)
        lam = jnp.exp(jnp.sum(lv[0] * lv[1])) - jnp.exp(jnp.sum(lv[2] * lv[3])) + lambda_init

        z_nr = _project(h, w_nr, BF16, geo, tn_cands=(2176, 256, 128))
        z_f = _project(h, w_f, F32, geo)
        z_rb = _project(h, w_rb, BF16, geo, epi="rope", rope=(cos_b, sin_b, scale_b, grp_b))
        z_rc = _project(h, w_rc, BF16, geo, epi="rope", rope=(cos_c, sin_c, scale_c, grp_c),
                        tn_cands=(1280, 256, 128))
        gates = _project(h, w_g, BF16, geo, epi="sigmoid")

        oa = _hgrn(z_nr, z_f, lower[:, layer], a_norm_g[layer], hconsts, geo)
        ob = _diff_attn(z_rb, z_nr, lam, b_subln_g[layer], lambda_init, geo)
        oc = _win_attn(z_rc, z_nr, c_sink[layer].astype(F32), geo)
        need_ctx = layer < depth - 1
        rows = xa.shape[0] if need_ctx else nl
        if need_ctx:
            ob = _diff_attn(z_rb, z_nr, lam, b_subln_g[layer], lambda_init, geo, prev=ob)
            oc = _win_attn(z_rc, z_nr, c_sink[layer].astype(F32), geo, prev=oc)

        m = _merge(oa, ob, oc, gates, w_branch[layer].astype(BF16), geo, rows)
        xa, h = _outproj(m, w_out[layer].astype(BF16), xa, g_post_mix[layer], modv(layer, 2),
                         g_pre_ffn[layer], modv(layer, 3), modv(layer, 4), geo, rows)
        act = _ffn_up(h, w_up[layer].astype(BF16), conv_w[layer], conv_b[layer], geo, rows)
        nxt = min(layer + 1, depth - 1)
        xa, h = _ffn_down(act, w_down[layer].astype(BF16), xa, g_post_ffn[layer], modv(layer, 5),
                          g_pre_mix[nxt], modv(nxt, 0), modv(nxt, 1), geo, rows)
    return xa.reshape(bsz, s, d)
```

```python
import functools
import math

import numpy as np
import jax
import jax.numpy as jnp
from jax import lax
from jax.experimental import pallas as pl
from jax.experimental.pallas import tpu as pltpu

F32 = jnp.float32
BF16 = jnp.bfloat16

D_MODEL = 2048
GRID_W = 64
EPS = 1e-6
ROPE_BASE = 10000.0
NEG_BIG = -1e30
BRANCH_W = D_MODEL // 2
A_HEADS = 8
A_DK = 128
A_DV = BRANCH_W // A_HEADS
B_HEADS = 8
B_DH = BRANCH_W // (2 * B_HEADS)
C_Q_HEADS = 8
C_KV_HEADS = 2
C_GROUP = C_Q_HEADS // C_KV_HEADS
C_DH = BRANCH_W // C_Q_HEADS
C_BLOCK = 128
D_FF = 5632
N_BRANCH = 3
LOG2E = 1.4426950408889634

LANES = 128
VMEM_LIMIT_BYTES = 56 * 1024 * 1024
MOD_ROWS = 16
HGRN_HEADS_PER_STEP = 4
ONES_ROWS = 16
SOFTMAX_COLS = 512


def _cparams(*sem):
    return pltpu.CompilerParams(dimension_semantics=sem, vmem_limit_bytes=VMEM_LIMIT_BYTES)


def _pick(cands, *dims):
    for c in cands:
        if all(d % c == 0 for d in dims):
            return c
    raise ValueError(f"no tile in {cands} divides {dims}")


def _dot(a, b):
    return jnp.dot(a, b, preferred_element_type=F32)


def _dot_nt(a, b):
    return lax.dot_general(a, b, (((1,), (1,)), ((), ())), preferred_element_type=F32)


def _dot_tn(a, b):
    return lax.dot_general(a, b, (((0,), (0,)), ((), ())), preferred_element_type=F32)


def _rms(x):
    return lax.rsqrt(jnp.mean(x * x, axis=-1, keepdims=True) + EPS)


def _silu(x):
    return x * jax.nn.sigmoid(x)


def _mod_kernel(c_ref, w_ref, b_ref, o_ref):
    s = _silu(c_ref[...])
    o_ref[0] = _dot(s.astype(BF16), w_ref[0].astype(BF16)) + b_ref[0]


def _modulation(cc, w_ada, b_ada):
    depth, d, n = w_ada.shape
    tn = _pick((1024, 512, 256, 128), n)
    return pl.pallas_call(
        _mod_kernel,
        grid=(depth, n // tn),
        in_specs=[pl.BlockSpec((MOD_ROWS, d), lambda l, j: (0, 0)),
                  pl.BlockSpec((1, d, tn), lambda l, j: (l, 0, j)),
                  pl.BlockSpec((1, 1, tn), lambda l, j: (l, 0, j))],
        out_specs=pl.BlockSpec((1, MOD_ROWS, tn), lambda l, j: (l, 0, j)),
        out_shape=jax.ShapeDtypeStruct((depth, MOD_ROWS, n), F32),
        compiler_params=_cparams("parallel", "parallel"),
        name="modulation",
    )(cc, w_ada, b_ada.reshape(depth, 1, n))


def _prenorm_kernel(x_ref, g_ref, sh_ref, sc_ref, h_ref):
    x = x_ref[...]
    h = (x * _rms(x) * g_ref[...]) * (1.0 + sc_ref[0]) + sh_ref[0]
    h_ref[...] = h.astype(h_ref.dtype)


def _prenorm(x, g, sh, sc, geo):
    t, d = x.shape
    tm = geo["tm_row"]
    midx = geo["midx"](tm)
    return pl.pallas_call(
        _prenorm_kernel,
        grid=(t // tm,),
        in_specs=[pl.BlockSpec((tm, d), lambda i: (i, 0)),
                  pl.BlockSpec((1, d), lambda i: (0, 0)),
                  pl.BlockSpec((1, 1, d), lambda i: (midx(i), 0, 0)),
                  pl.BlockSpec((1, 1, d), lambda i: (midx(i), 0, 0))],
        out_specs=pl.BlockSpec((tm, d), lambda i: (i, 0)),
        out_shape=jax.ShapeDtypeStruct((t, d), BF16),
        compiler_params=_cparams("parallel"),
        name="prenorm",
    )(x, g.reshape(1, d), sh, sc)


def _proj_kernel(h_ref, w_ref, *rest, epi, group):
    acc = _dot(h_ref[...], w_ref[...])
    if epi == "none":
        (o_ref,) = rest
        o_ref[...] = acc.astype(o_ref.dtype)
    elif epi == "sigmoid":
        (o_ref,) = rest
        o_ref[...] = jax.nn.sigmoid(acc).astype(o_ref.dtype)
    else:
        cos_ref, sin_ref, cs_ref, o_ref = rest
        tn = acc.shape[1]
        reps = tn // LANES
        cos = jnp.concatenate([cos_ref[...]] * reps, axis=1)
        sin = jnp.concatenate([sin_ref[...]] * reps, axis=1)
        lane = lax.broadcasted_iota(jnp.int32, acc.shape, 1)
        first = (lane % (2 * group)) < group
        partner = jnp.where(first, pltpu.roll(acc, tn - group, 1), pltpu.roll(acc, group, 1))
        o_ref[...] = ((acc * cos + partner * sin) * cs_ref[...]).astype(o_ref.dtype)


def _project(h, w, out_dtype, geo, epi="none", rope=None, tn_cands=(1024, 512, 256, 128)):
    t, d = h.shape
    n = w.shape[1]
    tm = geo["tm_mm"]
    tn = _pick(tn_cands, n)
    in_specs = [pl.BlockSpec((tm, d), lambda i, j: (i, 0)),
                pl.BlockSpec((d, tn), lambda i, j: (0, j))]
    args = [h, w]
    group = 0
    if epi == "rope":
        cos, sin, colscale, group = rope
        tab = geo["tabidx"](tm)
        in_specs += [pl.BlockSpec((tm, LANES), lambda i, j: (tab(i), 0)),
                     pl.BlockSpec((tm, LANES), lambda i, j: (tab(i), 0)),
                     pl.BlockSpec((1, tn), lambda i, j: (0, j))]
        args += [cos, sin, colscale]
    return pl.pallas_call(
        functools.partial(_proj_kernel, epi=epi, group=group),
        grid=(t // tm, n // tn),
        in_specs=in_specs,
        out_specs=pl.BlockSpec((tm, tn), lambda i, j: (i, j)),
        out_shape=jax.ShapeDtypeStruct((t, n), out_dtype),
        compiler_params=_cparams("parallel", "arbitrary"),
        name=f"proj_{epi}",
    )(*args)


def _hgrn_constants(c):
    nlev = int(math.log2(c))
    assert 1 << nlev == c
    t = np.arange(c)[:, None]
    u = np.arange(c)[None, :]
    w = np.zeros((2, (2 + nlev) * c + ONES_ROWS, c), np.float32)
    lev = np.full((2, c, c), -1, np.int32)
    for d in range(2):
        if d == 0:
            cum, rem = u <= t, u > t
        else:
            cum, rem = u >= t, u < t
        w[d, 0:c] = cum
        w[d, c:2 * c] = rem
        for l in range(nlev):
            h = 1 << l
            base = (t // (2 * h)) * (2 * h)
            if d == 0:
                m = base + h - 1
                e = np.where(t > m, (u > m) & (u <= t), (u > t) & (u <= m))
                pair = (t // (2 * h) == u // (2 * h)) & (t % (2 * h) >= h) & (u % (2 * h) < h)
            else:
                m = base + h
                e = np.where(t < m, (u >= t) & (u < m), (u >= m) & (u < t))
                pair = (t // (2 * h) == u // (2 * h)) & (t % (2 * h) < h) & (u % (2 * h) >= h)
            w[d, (2 + l) * c:(3 + l) * c] = e
            lev[d][pair] = l
        w[d, (2 + nlev) * c:] = 1.0
        lev[d][np.arange(c), np.arange(c)] = nlev
    return jnp.asarray(w, BF16), jnp.asarray(lev), nlev


def _hgrn_kernel(zqig_ref, zf_ref, lb_ref, ng_ref, w_ref, lev_ref, o_ref,
                 st_ref, ofwd_ref, *, nlev, n_ctx_blk, n_lat_blk):
    c = lev_ref.shape[1]
    assert zqig_ref.shape[0] == c
    gw = zf_ref.shape[1]
    half = c // 2
    d = pl.program_id(2)
    j = pl.program_id(3)

    @pl.when(j == 0)
    def _():
        st_ref[...] = jnp.zeros_like(st_ref)

    slot = jnp.where(d == 0, j,
                     jnp.where(j < n_ctx_blk, n_ctx_blk - 1 - j,
                               2 * n_ctx_blk + n_lat_blk - 1 - j))
    srow = pl.ds(pl.multiple_of(slot * c, c), c)
    halves = (slice(0, half), slice(half, c))

    def run(direction):
        lb = lb_ref[0]
        lev = lev_ref[0]
        low_levels = list(range(nlev - 1))
        masks = [{l: lev[r, r] == l for l in low_levels + [nlev]} for r in halves]
        zf = zf_ref[...]
        sig = jax.nn.sigmoid(zf)
        f = lb + (1.0 - lb) * sig
        kk = (1.0 - lb) * (1.0 - sig)
        g = jnp.log(f)
        a = jnp.exp(_dot(w_ref[0], g.astype(BF16)))
        q = _silu(zqig_ref[:, 0:gw].astype(F32))
        v = zqig_ref[:, gw:2 * gw]
        q_top, k_top = (halves[1], halves[0]) if direction == 0 else (halves[0], halves[1])
        for hh in range(HGRN_HEADS_PER_STEP):
            ls = slice(hh * LANES, (hh + 1) * LANES)
            qh, kh, vh = q[:, ls], kk[:, ls], v[:, ls]
            diag = []
            for r, mk in zip(halves, masks):
                qr, kr = qh[r], kh[r]
                sc = jnp.where(mk[nlev], _dot_nt(qr.astype(BF16), kr.astype(BF16)), 0.0)
                for l in low_levels:
                    al = a[(2 + l) * c:(3 + l) * c, ls][r]
                    sc = jnp.where(mk[l], _dot_nt((qr * al).astype(BF16), (kr * al).astype(BF16)), sc)
                diag.append(sc.astype(BF16))
            at = a[(1 + nlev) * c:(2 + nlev) * c, ls]
            off = _dot_nt((qh[q_top] * at[q_top]).astype(BF16), (kh[k_top] * at[k_top]).astype(BF16))
            st = st_ref[hh]
            stb = st.astype(BF16)
            qs = (qh * a[0:c, ls]).astype(BF16)
            outs = [_dot(diag[i], vh[r]) + _dot_nt(qs[r], stb) for i, r in enumerate(halves)]
            i_top = 1 if direction == 0 else 0
            outs[i_top] = outs[i_top] + _dot(off.astype(BF16), vh[k_top])
            ut = _dot_tn(vh, (kh * a[c:2 * c, ls]).astype(BF16))
            dec = a[(2 + nlev) * c:(2 + nlev) * c + 1, ls]
            st_ref[hh] = st * dec + ut
            o = jnp.concatenate(outs, axis=0)
            if direction == 0:
                ofwd_ref[srow, ls] = o
            else:
                tot = ofwd_ref[srow, ls] + o
                gate = _silu(zqig_ref[:, 2 * gw + hh * LANES:2 * gw + (hh + 1) * LANES].astype(F32))
                o_ref[:, ls] = (tot * _rms(tot) * ng_ref[:, ls] * gate).astype(o_ref.dtype)

    @pl.when(d == 0)
    def _():
        run(0)

    @pl.when(d == 1)
    def _():
        run(1)


def _hgrn(z_nr, z_f, lb, norm_g, consts, geo):
    w_c, lev_c, nlev = consts
    bsz, s, l = geo["B"], geo["S"], geo["L"]
    t = z_nr.shape[0]
    r = geo["hgrn_rows"]
    nc, nl = l // r, s // r
    wcols = HGRN_HEADS_PER_STEP * LANES
    hp = BRANCH_W // wcols
    ctx0 = bsz * s // r

    def rowblk(b, d, j):
        slot = jnp.where(d == 0, j, jnp.where(j < nc, nc - 1 - j, 2 * nc + nl - 1 - j))
        return jnp.where(slot < nc, ctx0 + b * nc + slot, b * nl + slot - nc)

    def outmap(b, p, d, j):
        return (rowblk(b, 1, jnp.where(d == 0, 0, j)), p)

    kern = functools.partial(_hgrn_kernel, nlev=nlev, n_ctx_blk=nc, n_lat_blk=nl)
    return pl.pallas_call(
        kern,
        grid=(bsz, hp, 2, nc + nl),
        in_specs=[pl.BlockSpec((r, 3 * wcols), lambda b, p, d, j: (rowblk(b, d, j), p)),
                  pl.BlockSpec((r, wcols), lambda b, p, d, j: (rowblk(b, d, j), d * hp + p)),
                  pl.BlockSpec((1, 1, wcols), lambda b, p, d, j: (d, 0, p)),
                  pl.BlockSpec((1, wcols), lambda b, p, d, j: (0, p)),
                  pl.BlockSpec((1,) + w_c.shape[1:], lambda b, p, d, j: (d, 0, 0)),
                  pl.BlockSpec((1,) + lev_c.shape[1:], lambda b, p, d, j: (d, 0, 0))],
        out_specs=pl.BlockSpec((r, wcols), outmap),
        out_shape=jax.ShapeDtypeStruct((t, BRANCH_W), BF16),
        scratch_shapes=[pltpu.VMEM((HGRN_HEADS_PER_STEP, A_DV, A_DK), F32),
                        pltpu.VMEM((s + l, wcols), F32)],
        compiler_params=_cparams("parallel", "parallel", "arbitrary", "arbitrary"),
        name="hgrn2",
    )(z_nr, z_f, lb.reshape(2, 1, BRANCH_W),
      jnp.tile(norm_g, A_HEADS).reshape(1, BRANCH_W), w_c, lev_c)


def _diff_attn_kernel(lam_ref, q_ref, *rest, n_seg, out_scale):
    kv = rest[:2 * n_seg]
    g_ref, o_ref = rest[2 * n_seg], rest[2 * n_seg + 1]
    q = q_ref[...]
    lane = lax.broadcasted_iota(jnp.int32, q.shape, 1)
    zero = jnp.zeros_like(q)
    qs = (jnp.where(lane < B_DH, q, zero), jnp.where(lane >= B_DH, q, zero))
    es, inv = [], []
    for comp in range(2):
        s = [_dot_nt(qs[comp], kv[2 * i][...]) for i in range(n_seg)]
        m = s[0].max(axis=-1, keepdims=True)
        for x in s[1:]:
            m = jnp.maximum(m, x.max(axis=-1, keepdims=True))
        e = [jnp.exp2(x - m) for x in s]
        tot = e[0].sum(axis=-1, keepdims=True)
        for x in e[1:]:
            tot = tot + x.sum(axis=-1, keepdims=True)
        es.append(e)
        inv.append(1.0 / tot)
    r0 = inv[0]
    r1 = inv[1] * lam_ref[0, 0]
    o = None
    for i in range(n_seg):
        wgt = (es[0][i] * r0 - es[1][i] * r1).astype(BF16)
        part = _dot(wgt, kv[2 * i + 1][...])
        o = part if o is None else o + part
    o_ref[...] = (o * _rms(o) * (g_ref[...] * out_scale)).astype(o_ref.dtype)


def _diff_attn_pipe_kernel(lam_ref, q_ref, kl_ref, vl_ref, kc_ref, vc_ref, g_ref, o_ref,
                           s0_ref, s1_ref, m0_ref, m1_ref, e0_ref, e1_ref, wa_ref, wb_ref, ra_ref, rb_ref, *, tq, out_scale):
    s_len = kl_ref.shape[0]
    n = q_ref.shape[0] // tq
    lam = lam_ref[0, 0]
    gain = g_ref[...] * out_scale
    sa_ref, sb_ref = (s0_ref, m0_ref, e0_ref), (s1_ref, m1_ref, e1_ref)

    def rows(j):
        return pl.ds(pl.multiple_of(j * tq, tq), tq)

    def scores(j, sm_refs):
        s_ref, m_ref, _ = sm_refs
        q = q_ref[rows(j), :]
        lane = lax.broadcasted_iota(jnp.int32, q.shape, 1)
        zero = jnp.zeros_like(q)
        for comp, qc in enumerate((jnp.where(lane < B_DH, q, zero), jnp.where(lane >= B_DH, q, zero))):
            s_lat = _dot_nt(qc, kl_ref[...])
            s_ctx = _dot_nt(qc, kc_ref[...])
            s_ref[comp, :, 0:s_len] = s_lat
            s_ref[comp, :, s_len:] = s_ctx
            m_ref[comp] = jnp.maximum(s_lat.max(axis=-1, keepdims=True), s_ctx.max(axis=-1, keepdims=True))

    nk = s_len + kc_ref.shape[0]
    col_chunks = [(c0, min(c0 + SOFTMAX_COLS, nk)) for c0 in range(0, nk, SOFTMAX_COLS)]

    def softmax(sm_refs, w_ref, r_ref):
        s_ref, m_ref, e_ref = sm_refs
        ls = []
        for comp in range(2):
            m = m_ref[comp]
            acc = jnp.zeros((tq, LANES), F32)
            for c0, c1 in col_chunks:
                e = jnp.exp2(s_ref[comp, :, c0:c1] - m)
                for k in range(0, c1 - c0, LANES):
                    acc = acc + e[:, k:k + LANES]
                e_ref[comp, :, c0:c1] = e.astype(BF16)
            ls.append(acc.sum(axis=-1, keepdims=True))
        coef = (lam * ls[0] / ls[1]).astype(BF16)
        for c0, c1 in col_chunks:
            w_ref[:, c0:c1] = e_ref[0, :, c0:c1] - coef * e_ref[1, :, c0:c1]
        r_ref[...] = 1.0 / ls[0]

    def values(j, w_ref, r_ref):
        o = (_dot(w_ref[:, 0:s_len], vl_ref[...]) + _dot(w_ref[:, s_len:], vc_ref[...])) * r_ref[...]
        o_ref[rows(j), :] = (o * _rms(o) * gain).astype(o_ref.dtype)

    scores(0, sa_ref)
    scores(1, sb_ref)
    softmax(sa_ref, wa_ref, ra_ref)

    def pair(u, carry):
        t = 2 * u
        values(t - 2, wa_ref, ra_ref)
        softmax(sb_ref, wb_ref, rb_ref)
        scores(t, sa_ref)
        values(t - 1, wb_ref, rb_ref)
        softmax(sa_ref, wa_ref, ra_ref)
        scores(t + 1, sb_ref)
        return carry

    lax.fori_loop(1, n // 2, pair, 0)
    softmax(sb_ref, wb_ref, rb_ref)
    values(n - 2, wa_ref, ra_ref)
    values(n - 1, wb_ref, rb_ref)


def _diff_attn_lat(z_rb, z_nr, lam, subln_g, lambda_init, geo):
    bsz, s, l = geo["B"], geo["S"], geo["L"]
    t = z_rb.shape[0]
    ctx0 = bsz * s // l
    kcol, vcol = B_HEADS, 3 * B_HEADS
    tq = geo["tq_b"]
    assert (s // tq) % 2 == 0
    nk = s + l
    return pl.pallas_call(
        functools.partial(_diff_attn_pipe_kernel, tq=tq, out_scale=1.0 - lambda_init),
        grid=(bsz, B_HEADS),
        in_specs=[pl.BlockSpec(memory_space=pltpu.SMEM),
                  pl.BlockSpec((s, LANES), lambda b, h: (b, h)),
                  pl.BlockSpec((s, LANES), lambda b, h: (b, kcol + h)),
                  pl.BlockSpec((s, LANES), lambda b, h: (b, vcol + h)),
                  pl.BlockSpec((l, LANES), lambda b, h: (ctx0 + b, kcol + h)),
                  pl.BlockSpec((l, LANES), lambda b, h: (ctx0 + b, vcol + h)),
                  pl.BlockSpec((1, LANES), lambda b, h: (0, 0))],
        out_specs=pl.BlockSpec((s, LANES), lambda b, h: (b, h)),
        out_shape=jax.ShapeDtypeStruct((t, BRANCH_W), BF16),
        scratch_shapes=[pltpu.VMEM((2, tq, nk), F32), pltpu.VMEM((2, tq, nk), F32),
                        pltpu.VMEM((2, tq, 1), F32), pltpu.VMEM((2, tq, 1), F32),
                        pltpu.VMEM((2, tq, nk), BF16), pltpu.VMEM((2, tq, nk), BF16),
                        pltpu.VMEM((tq, nk), BF16), pltpu.VMEM((tq, nk), BF16),
                        pltpu.VMEM((tq, 1), F32), pltpu.VMEM((tq, 1), F32)],
        compiler_params=_cparams("parallel", "parallel"),
        name="diff_attn_lat",
    )(lam.reshape(1, 1), z_rb, z_rb, z_nr, z_rb, z_nr, subln_g.reshape(1, 2 * B_DH))


def _diff_attn(z_rb, z_nr, lam, subln_g, lambda_init, geo, prev=None):
    bsz, s, l = geo["B"], geo["S"], geo["L"]
    t = z_rb.shape[0]
    ctx0 = bsz * s // l
    kcol, vcol = B_HEADS, 3 * B_HEADS
    g2 = subln_g.reshape(1, 2 * B_DH)
    lam2 = lam.reshape(1, 1)
    if prev is None:
        return _diff_attn_lat(z_rb, z_nr, lam, subln_g, lambda_init, geo)

    def kern(lam_ref, q_ref, k_ref, v_ref, g_ref, prev_ref, o_ref):
        del prev_ref
        _diff_attn_kernel(lam_ref, q_ref, k_ref, v_ref, g_ref, o_ref, n_seg=1, out_scale=1.0 - lambda_init)

    return pl.pallas_call(
        kern,
        grid=(bsz, B_HEADS),
        in_specs=[pl.BlockSpec(memory_space=pltpu.SMEM),
                  pl.BlockSpec((l, LANES), lambda b, h: (ctx0 + b, h)),
                  pl.BlockSpec((l, LANES), lambda b, h: (ctx0 + b, kcol + h)),
                  pl.BlockSpec((l, LANES), lambda b, h: (ctx0 + b, vcol + h)),
                  pl.BlockSpec((1, LANES), lambda b, h: (0, 0)),
                  pl.BlockSpec(memory_space=pl.ANY)],
        out_specs=pl.BlockSpec((l, LANES), lambda b, h: (ctx0 + b, h)),
        out_shape=jax.ShapeDtypeStruct((t, BRANCH_W), BF16),
        input_output_aliases={5: 0},
        compiler_params=_cparams("parallel", "parallel"),
        name="diff_attn_ctx",
    )(lam2, z_rb, z_rb, z_nr, g2, prev)


def _stack_heads(q_ref, r0, rows):
    return jnp.concatenate([q_ref[r0:r0 + rows, r * LANES:(r + 1) * LANES] for r in range(C_GROUP)], axis=0)


def _sink_column(sink_ref, g, rows_per_head):
    rid = lax.broadcasted_iota(jnp.int32, (C_GROUP * rows_per_head, 1), 0) // rows_per_head
    col = jnp.full(rid.shape, sink_ref[g * C_GROUP] * LOG2E, F32)
    for r in range(1, C_GROUP):
        col = jnp.where(rid == r, sink_ref[g * C_GROUP + r] * LOG2E, col)
    return col


def _lane_groups(x):
    return [x[:, k:k + LANES] for k in range(0, x.shape[1], LANES)]


def _sink_attend(scores, values, sink_col, o_ref, r0, rows_per_head):
    groups = [grp for x in scores for grp in _lane_groups(x)]
    mvec = groups[0]
    for grp in groups[1:]:
        mvec = jnp.maximum(mvec, grp)
    m = jnp.maximum(sink_col, mvec.max(axis=-1, keepdims=True))
    e = [jnp.exp2(x - m) for x in scores]
    egroups = [grp for x in e for grp in _lane_groups(x)]
    svec = egroups[0]
    for grp in egroups[1:]:
        svec = svec + grp
    inv = 1.0 / (jnp.exp2(sink_col - m) + svec.sum(axis=-1, keepdims=True))
    o = None
    for x, v in zip(e, values):
        part = _dot(x.astype(BF16), v)
        o = part if o is None else o + part
    o = o * inv
    for r in range(C_GROUP):
        o_ref[r0:r0 + rows_per_head, r * LANES:(r + 1) * LANES] = (
            o[r * rows_per_head:(r + 1) * rows_per_head].astype(o_ref.dtype))


def _win_attn_kernel(sink_ref, q_ref, kc_ref, vc_ref, kp_ref, km_ref, kn_ref, vp_ref, vm_ref, vn_ref,
                     o_ref, *, n_step):
    g = pl.program_id(1)
    n = pl.program_id(2)
    n_sub = q_ref.shape[0] // C_BLOCK
    rows = C_GROUP * C_BLOCK
    tq = lax.broadcasted_iota(jnp.int32, (rows, C_BLOCK), 0) % C_BLOCK
    kj = lax.broadcasted_iota(jnp.int32, (rows, C_BLOCK), 1)
    sink_col = _sink_column(sink_ref, g, C_BLOCK)

    def blk(ref, a):
        return ref[a * C_BLOCK:(a + 1) * C_BLOCK, :]

    kband = [kp_ref[...]] + [blk(km_ref, a) for a in range(n_sub)] + [kn_ref[...]]
    vband = [vp_ref[...]] + [blk(vm_ref, a) for a in range(n_sub)] + [vn_ref[...]]
    for a in range(n_sub):
        q4 = _stack_heads(q_ref, a * C_BLOCK, C_BLOCK)
        has_prev = True if a > 0 else n > 0
        has_next = True if a < n_sub - 1 else n < n_step - 1
        sp = jnp.where(jnp.logical_and(kj >= tq, has_prev), _dot_nt(q4, kband[a]), NEG_BIG)
        sn = jnp.where(jnp.logical_and(kj <= tq, has_next), _dot_nt(q4, kband[a + 2]), NEG_BIG)
        scores = [_dot_nt(q4, kc_ref[...]), sp, _dot_nt(q4, kband[a + 1]), sn]
        values = [vc_ref[...], vband[a], vband[a + 1], vband[a + 2]]
        _sink_attend(scores, values, sink_col, o_ref, a * C_BLOCK, C_BLOCK)


def _win_attn_ctx_kernel(sink_ref, q_ref, kc_ref, vc_ref, prev_ref, o_ref):
    del prev_ref
    g = pl.program_id(1)
    rows_per_head = q_ref.shape[0]
    q4 = _stack_heads(q_ref, 0, rows_per_head)
    _sink_attend([_dot_nt(q4, kc_ref[...])], [vc_ref[...]],
                 _sink_column(sink_ref, g, rows_per_head), o_ref, 0, rows_per_head)


def _win_attn(z_rc, z_nr, sink, geo, prev=None):
    bsz, s, l = geo["B"], geo["S"], geo["L"]
    t = z_rc.shape[0]
    n_blk = s // C_BLOCK
    qw = C_GROUP * LANES
    kcol = C_Q_HEADS
    vcol = (4 * BRANCH_W) // LANES
    ctx0 = bsz * s // l
    smem = pl.BlockSpec(memory_space=pltpu.SMEM)
    if prev is None:
        qb = geo["win_rows"]
        sub = qb // C_BLOCK
        n_step = s // qb

        def edge(colblk0, shift):
            def imap(b, g, n):
                return (b * n_blk + jnp.clip(n * sub + shift, 0, n_blk - 1), colblk0 + g)
            return pl.BlockSpec((C_BLOCK, LANES), imap)

        def mid(colblk0):
            return pl.BlockSpec((qb, LANES), lambda b, g, n: (b * n_step + n, colblk0 + g))

        in_specs = [smem,
                    pl.BlockSpec((qb, qw), lambda b, g, n: (b * n_step + n, g)),
                    pl.BlockSpec((l, LANES), lambda b, g, n: (ctx0 + b, kcol + g)),
                    pl.BlockSpec((l, LANES), lambda b, g, n: (ctx0 + b, vcol + g)),
                    edge(kcol, -1), mid(kcol), edge(kcol, sub),
                    edge(vcol, -1), mid(vcol), edge(vcol, sub)]
        args = [sink, z_rc, z_rc, z_nr, z_rc, z_rc, z_rc, z_nr, z_nr, z_nr]
        return pl.pallas_call(
            functools.partial(_win_attn_kernel, n_step=n_step),
            grid=(bsz, C_KV_HEADS, n_step),
            in_specs=in_specs,
            out_specs=pl.BlockSpec((qb, qw), lambda b, g, n: (b * n_step + n, g)),
            out_shape=jax.ShapeDtypeStruct((t, BRANCH_W), BF16),
            compiler_params=_cparams("parallel", "parallel", "arbitrary"),
            name="win_attn_lat",
        )(*args)
    in_specs = [smem,
                pl.BlockSpec((l, qw), lambda b, g: (ctx0 + b, g)),
                pl.BlockSpec((l, LANES), lambda b, g: (ctx0 + b, kcol + g)),
                pl.BlockSpec((l, LANES), lambda b, g: (ctx0 + b, vcol + g)),
                pl.BlockSpec(memory_space=pl.ANY)]
    return pl.pallas_call(
        _win_attn_ctx_kernel,
        grid=(bsz, C_KV_HEADS),
        in_specs=in_specs,
        out_specs=pl.BlockSpec((l, qw), lambda b, g: (ctx0 + b, g)),
        out_shape=jax.ShapeDtypeStruct((t, BRANCH_W), BF16),
        input_output_aliases={4: 0},
        compiler_params=_cparams("parallel", "parallel"),
        name="win_attn_ctx",
    )(sink, z_rc, z_rc, z_nr, prev)


def _merge_kernel(a_ref, b_ref, c_ref, w_ref, ga_ref, gb_ref, gc_ref, o_ref):
    acc = ga_ref[...].astype(F32) * _dot(a_ref[...], w_ref[0])
    acc = acc + gb_ref[...].astype(F32) * _dot(b_ref[...], w_ref[1])
    acc = acc + gc_ref[...].astype(F32) * _dot(c_ref[...], w_ref[2])
    o_ref[...] = acc.astype(o_ref.dtype)


def _merge(oa, ob, oc, gates, w_branch, geo, rows):
    t = rows
    tm = geo["tm_mm"]
    tn = 512
    nj = D_MODEL // tn
    br = pl.BlockSpec((tm, BRANCH_W), lambda i, j: (i, 0))

    def gate(n):
        return pl.BlockSpec((tm, tn), lambda i, j: (i, n * nj + j))

    return pl.pallas_call(
        _merge_kernel,
        grid=(t // tm, nj),
        in_specs=[br, br, br,
                  pl.BlockSpec((N_BRANCH, BRANCH_W, tn), lambda i, j: (0, 0, j)),
                  gate(0), gate(1), gate(2)],
        out_specs=pl.BlockSpec((tm, tn), lambda i, j: (i, j)),
        out_shape=jax.ShapeDtypeStruct((t, D_MODEL), BF16),
        compiler_params=_cparams("parallel", "arbitrary"),
        name="branch_merge",
    )(oa, ob, oc, w_branch, gates, gates, gates)


def _residual_epilogue(y, x_ref, gpost_ref, gate_ref, gpre_ref, sh_ref, sc_ref, xo_ref, ho_ref):
    x1 = x_ref[...] + gate_ref[0] * (y * _rms(y) * gpost_ref[...])
    xo_ref[...] = x1
    h = (x1 * _rms(x1) * gpre_ref[...]) * (1.0 + sc_ref[0]) + sh_ref[0]
    ho_ref[...] = h.astype(ho_ref.dtype)


def _outproj_kernel(m_ref, w_ref, *rest):
    _residual_epilogue(_dot(m_ref[...], w_ref[...]), *rest)


def _down_kernel(a_ref, w_ref, x_ref, gpost_ref, gate_ref, gpre_ref, sh_ref, sc_ref, xo_ref, ho_ref, y_ref):
    j = pl.program_id(1)
    nj = y_ref.shape[0]
    y_ref[j] = _dot(a_ref[...], w_ref[...])

    @pl.when(j == nj - 1)
    def _():
        y = jnp.concatenate([y_ref[jj] for jj in range(nj)], axis=1)
        _residual_epilogue(y, x_ref, gpost_ref, gate_ref, gpre_ref, sh_ref, sc_ref, xo_ref, ho_ref)


def _residual_specs(tm, d, midx, grid_rank):
    if grid_rank == 1:
        row = lambda i: (i, 0)
        const = lambda i: (0, 0)
        mod = lambda i: (midx(i), 0, 0)
    else:
        row = lambda i, k: (i, 0)
        const = lambda i, k: (0, 0)
        mod = lambda i, k: (midx(i), 0, 0)
    ins = [pl.BlockSpec((tm, d), row), pl.BlockSpec((1, d), const), pl.BlockSpec((1, 1, d), mod),
           pl.BlockSpec((1, d), const), pl.BlockSpec((1, 1, d), mod), pl.BlockSpec((1, 1, d), mod)]
    outs = [pl.BlockSpec((tm, d), row), pl.BlockSpec((tm, d), row)]
    return ins, outs


def _outproj(m, w, x, gpost, gate, gpre, sh, sc, geo, rows):
    t, d = rows, x.shape[1]
    tm = geo["tm_row"]
    r_ins, r_outs = _residual_specs(tm, d, geo["midx"](tm), 1)
    return pl.pallas_call(
        _outproj_kernel,
        grid=(t // tm,),
        in_specs=[pl.BlockSpec((tm, d), lambda i: (i, 0)), pl.BlockSpec((d, d), lambda i: (0, 0))] + r_ins,
        out_specs=r_outs,
        out_shape=[jax.ShapeDtypeStruct((t, d), F32), jax.ShapeDtypeStruct((t, d), BF16)],
        compiler_params=_cparams("parallel"),
        name="outproj_residual",
    )(m, w, x, gpost.reshape(1, d), gate, gpre.reshape(1, d), sh, sc)


def _ffn_down(act, w, x, gpost, gate, gpre, sh, sc, geo, rows):
    t, d = rows, x.shape[1]
    kf = act.shape[1]
    tm = geo["tm_row"]
    tn = _pick((512, 256, 128), d)
    r_ins, r_outs = _residual_specs(tm, d, geo["midx"](tm), 2)
    return pl.pallas_call(
        _down_kernel,
        grid=(t // tm, d // tn),
        in_specs=[pl.BlockSpec((tm, kf), lambda i, j: (i, 0)), pl.BlockSpec((kf, tn), lambda i, j: (0, j))] + r_ins,
        out_specs=r_outs,
        out_shape=[jax.ShapeDtypeStruct((t, d), F32), jax.ShapeDtypeStruct((t, d), BF16)],
        scratch_shapes=[pltpu.VMEM((d // tn, tm, tn), F32)],
        compiler_params=_cparams("parallel", "arbitrary"),
        name="ffn_down_residual",
    )(act, w, x, gpost.reshape(1, d), gate, gpre.reshape(1, d), sh, sc)


HALO = 16


def _ffn_up_kernel(hp_ref, h_ref, hn_ref, wa_ref, wv_ref, cwa_ref, cwv_ref, cba_ref, cbv_ref, o_ref,
                   *, n_lat_blk, s, l):
    i = pl.program_id(0)
    tm = h_ref.shape[0]
    hext = jnp.concatenate([hp_ref[...], h_ref[...], hn_ref[...]], axis=0)
    row = lax.broadcasted_iota(jnp.int32, (tm, 1), 0)

    def edges(blk, seq):
        if seq >= tm:
            per = seq // tm
            return tm - 1, blk % per == 0, blk % per == per - 1
        assert seq & (seq - 1) == 0, "short sequences must have power-of-two length"
        return seq - 1, True, True

    is_lat = i < n_lat_blk
    pm_l, bf_l, bl_l = edges(i, s)
    pm_c, bf_c, bl_c = edges(i - n_lat_blk, l)
    pm = jnp.where(is_lat, pm_l, pm_c)
    is_first = jnp.logical_and((row & pm) == 0, jnp.where(is_lat, bf_l, bf_c))
    is_last = jnp.logical_and((row & pm) == pm, jnp.where(is_lat, bl_l, bl_c))

    def conv(w_ref, cw_ref, cb_ref):
        u = _dot(hext, w_ref[...])
        prev = jnp.where(is_first, 0.0, u[HALO - 1:HALO - 1 + tm])
        nxt = jnp.where(is_last, 0.0, u[HALO + 1:HALO + 1 + tm])
        cw = cw_ref[...]
        return prev * cw[0:1] + u[HALO:HALO + tm] * cw[1:2] + nxt * cw[2:3] + cb_ref[...]

    a = conv(wa_ref, cwa_ref, cba_ref)
    v = conv(wv_ref, cwv_ref, cbv_ref)
    o_ref[...] = (_silu(a) * v).astype(o_ref.dtype)


def _ffn_up(h, w_up, conv_w, conv_b, geo, rows):
    t, d = rows, h.shape[1]
    tm = geo["tm_mm"]
    tn = _pick((512, 256, 128), D_FF)
    nj = D_FF // tn
    n_halo = t // HALO
    per = tm // HALO
    cb = conv_b.reshape(1, 2 * D_FF)
    kern = functools.partial(_ffn_up_kernel, n_lat_blk=geo["B"] * geo["S"] // tm, s=geo["S"], l=geo["L"])
    return pl.pallas_call(
        kern,
        grid=(t // tm, nj),
        in_specs=[pl.BlockSpec((HALO, d), lambda i, j: (jnp.maximum(i * per - 1, 0), 0)),
                  pl.BlockSpec((tm, d), lambda i, j: (i, 0)),
                  pl.BlockSpec((HALO, d), lambda i, j: (jnp.minimum((i + 1) * per, n_halo - 1), 0)),
                  pl.BlockSpec((d, tn), lambda i, j: (0, j)),
                  pl.BlockSpec((d, tn), lambda i, j: (0, nj + j)),
                  pl.BlockSpec((3, tn), lambda i, j: (0, j)),
                  pl.BlockSpec((3, tn), lambda i, j: (0, nj + j)),
                  pl.BlockSpec((1, tn), lambda i, j: (0, j)),
                  pl.BlockSpec((1, tn), lambda i, j: (0, nj + j))],
        out_specs=pl.BlockSpec((tm, tn), lambda i, j: (i, j)),
        out_shape=jax.ShapeDtypeStruct((t, D_FF), BF16),
        compiler_params=_cparams("parallel", "arbitrary"),
        name="ffn_up_conv",
    )(h, h, h, w_up, w_up, conv_w, conv_w, cb, cb)


def _rope_tables(s, head_dim, tm):
    quarter = head_dim // 4
    pos = np.arange(s)
    inv = np.power(np.float32(ROPE_BASE), -np.arange(quarter, dtype=np.float32) / quarter).astype(np.float32)
    row = (pos // GRID_W).astype(np.float32)[:, None] * inv
    col = (pos % GRID_W).astype(np.float32)[:, None] * inv
    ang = jnp.asarray(np.concatenate([row, row, col, col], axis=1))
    sign = np.concatenate([-np.ones(quarter), np.ones(quarter)] * 2).astype(np.float32)
    cos = jnp.cos(ang)
    sin = jnp.sin(ang) * sign
    reps = LANES // head_dim
    cos = jnp.tile(cos, (1, reps))
    sin = jnp.tile(sin, (1, reps))
    cos = jnp.concatenate([cos, jnp.ones((tm, LANES), F32)], axis=0)
    sin = jnp.concatenate([sin, jnp.zeros((tm, LANES), F32)], axis=0)
    return cos, sin, quarter


def _geometry(bsz, s, l):
    nl, nc = bsz * s, bsz * l

    def midx(tm):
        nlat, per = nl // tm, s // tm
        return lambda i: jnp.where(i < nlat, 1 + i // per, 0)

    def tabidx(tm):
        nlat, per = nl // tm, s // tm
        return lambda i: jnp.where(i < nlat, i % per, per)

    return {"B": bsz, "S": s, "L": l,
            "tm_mm": _pick((1024, 512, 256, 128), s, nc),
            "tm_row": _pick((512, 256, 128), s, nc),
            "tq_b": _pick((256, 128, 64), s // 2),
            "hgrn_rows": _pick((256, 128, 64), s, l),
            "win_rows": _pick((512, 256, 128), s),
            "midx": midx, "tabidx": tabidx}


def kernel(x, c, ctx, c_ctx, w_ada, b_ada, g_pre_mix, g_post_mix, g_pre_ffn, g_post_ffn, w_in, a_lb_logits, a_norm_g, b_lambda, b_subln_g, c_sink, w_branch, w_out, w_up, conv_w, conv_b, w_down):
    bsz, s, d = x.shape
    l = ctx.shape[1]
    depth = w_in.shape[0]
    assert d == D_MODEL and s % GRID_W == 0 and s % C_BLOCK == 0 and bsz + 1 <= MOD_ROWS
    geo = _geometry(bsz, s, l)
    nl = bsz * s

    xa = jnp.concatenate([x.reshape(nl, d), ctx.reshape(bsz * l, d)], axis=0)

    cc = jnp.concatenate([c_ctx[None], c, jnp.zeros((MOD_ROWS - 1 - bsz, d), F32)], axis=0)
    mod = _modulation(cc, w_ada, b_ada).reshape(depth, MOD_ROWS, 6, 1, d)

    def modv(layer, k):
        return mod[layer, :, k]

    p = jax.nn.softmax(a_lb_logits.astype(F32), axis=1)
    lower = jnp.cumsum(p, axis=1) - p[:, :1]
    hconsts = _hgrn_constants(geo["hgrn_rows"])

    tm = geo["tm_mm"]
    cos_b, sin_b, grp_b = _rope_tables(s, B_DH, tm)
    cos_c, sin_c, grp_c = _rope_tables(s, C_DH, tm)
    scale_b = jnp.concatenate([jnp.full((BRANCH_W,), B_DH ** -0.5 * LOG2E, F32),
                               jnp.ones((BRANCH_W,), F32)]).reshape(1, -1)
    scale_c = jnp.concatenate([jnp.full((BRANCH_W,), C_DH ** -0.5 * LOG2E, F32),
                               jnp.ones((C_KV_HEADS * C_DH,), F32)]).reshape(1, -1)

    e = np.cumsum((0, 1024, 1024, 1024, 1024, 1024, 1024, 1024, 1024, 1024, 256, 256, 6144))
    h = _prenorm(xa, g_pre_mix[0], modv(0, 0), modv(0, 1), geo)
    for layer in range(depth):
        wi = w_in[layer]
        gw = HGRN_HEADS_PER_STEP * LANES
        a_cols = [wi[:, e[k] + p * gw:e[k] + (p + 1) * gw] for p in range(BRANCH_W // gw) for k in (0, 3, 4)]
        w_nr = jnp.concatenate(a_cols + [wi[:, e[7]:e[8]], wi[:, e[10]:e[11]]], axis=1).astype(BF16)
        w_f = wi[:, e[1]:e[3]].astype(BF16)
        w_rb = wi[:, e[5]:e[7]].astype(BF16)
        w_rc = wi[:, e[8]:e[10]].astype(BF16)
        w_g = wi[:, e[11]:e[12]].astype(BF16)
        lambda_init = 0.8 - 0.6 * math.exp(-0.3 * layer)
        lv = b_lambda[layer].astype(F32)
        lam = jnp.exp(jnp.sum(lv[0] * lv[1])) - jnp.exp(jnp.sum(lv[2] * lv[3])) + lambda_init

        z_nr = _project(h, w_nr, BF16, geo, tn_cands=(2176, 256, 128))
        z_f = _project(h, w_f, F32, geo)
        z_rb = _project(h, w_rb, BF16, geo, epi="rope", rope=(cos_b, sin_b, scale_b, grp_b))
        z_rc = _project(h, w_rc, BF16, geo, epi="rope", rope=(cos_c, sin_c, scale_c, grp_c),
                        tn_cands=(1280, 256, 128))
        gates = _project(h, w_g, BF16, geo, epi="sigmoid")

        oa = _hgrn(z_nr, z_f, lower[:, layer], a_norm_g[layer], hconsts, geo)
        ob = _diff_attn(z_rb, z_nr, lam, b_subln_g[layer], lambda_init, geo)
        oc = _win_attn(z_rc, z_nr, c_sink[layer].astype(F32), geo)
        need_ctx = layer < depth - 1
        rows = xa.shape[0] if need_ctx else nl
        if need_ctx:
            ob = _diff_attn(z_rb, z_nr, lam, b_subln_g[layer], lambda_init, geo, prev=ob)
            oc = _win_attn(z_rc, z_nr, c_sink[layer].astype(F32), geo, prev=oc)

        m = _merge(oa, ob, oc, gates, w_branch[layer].astype(BF16), geo, rows)
        xa, h = _outproj(m, w_out[layer].astype(BF16), xa, g_post_mix[layer], modv(layer, 2),
                         g_pre_ffn[layer], modv(layer, 3), modv(layer, 4), geo, rows)
        act = _ffn_up(h, w_up[layer].astype(BF16), conv_w[layer], conv_b[layer], geo, rows)
        nxt = min(layer + 1, depth - 1)
        xa, h = _ffn_down(act, w_down[layer].astype(BF16), xa, g_post_ffn[layer], modv(layer, 5),
                          g_pre_mix[nxt], modv(nxt, 0), modv(nxt, 1), geo, rows)
    return xa.reshape(bsz, s, d)
```

```python
import functools
import math

import numpy as np
import jax
import jax.numpy as jnp
from jax import lax
from jax.experimental import pallas as pl
from jax.experimental.pallas import tpu as pltpu

F32 = jnp.float32
BF16 = jnp.bfloat16

D_MODEL = 2048
GRID_W = 64
EPS = 1e-6
ROPE_BASE = 10000.0
NEG_BIG = -1e30
BRANCH_W = D_MODEL // 2
A_HEADS = 8
A_DK = 128
A_DV = BRANCH_W // A_HEADS
B_HEADS = 8
B_DH = BRANCH_W // (2 * B_HEADS)
C_Q_HEADS = 8
C_KV_HEADS = 2
C_GROUP = C_Q_HEADS // C_KV_HEADS
C_DH = BRANCH_W // C_Q_HEADS
C_BLOCK = 128
D_FF = 5632
N_BRANCH = 3
LOG2E = 1.4426950408889634

LANES = 128
VMEM_LIMIT_BYTES = 56 * 1024 * 1024
MOD_ROWS = 16
HGRN_HEADS_PER_STEP = 4
ONES_ROWS = 16
SOFTMAX_COLS = 512


def _cparams(*sem):
    return pltpu.CompilerParams(dimension_semantics=sem, vmem_limit_bytes=VMEM_LIMIT_BYTES)


def _pick(cands, *dims):
    for c in cands:
        if all(d % c == 0 for d in dims):
            return c
    raise ValueError(f"no tile in {cands} divides {dims}")


def _dot(a, b):
    return jnp.dot(a, b, preferred_element_type=F32)


def _dot_nt(a, b):
    return lax.dot_general(a, b, (((1,), (1,)), ((), ())), preferred_element_type=F32)


def _dot_tn(a, b):
    return lax.dot_general(a, b, (((0,), (0,)), ((), ())), preferred_element_type=F32)


def _rms(x):
    return lax.rsqrt(jnp.mean(x * x, axis=-1, keepdims=True) + EPS)


def _silu(x):
    return x * jax.nn.sigmoid(x)


def _mod_kernel(c_ref, w_ref, b_ref, o_ref):
    s = _silu(c_ref[...])
    o_ref[0] = _dot(s.astype(BF16), w_ref[0].astype(BF16)) + b_ref[0]


def _modulation(cc, w_ada, b_ada):
    depth, d, n = w_ada.shape
    tn = _pick((1024, 512, 256, 128), n)
    return pl.pallas_call(
        _mod_kernel,
        grid=(depth, n // tn),
        in_specs=[pl.BlockSpec((MOD_ROWS, d), lambda l, j: (0, 0)),
                  pl.BlockSpec((1, d, tn), lambda l, j: (l, 0, j)),
                  pl.BlockSpec((1, 1, tn), lambda l, j: (l, 0, j))],
        out_specs=pl.BlockSpec((1, MOD_ROWS, tn), lambda l, j: (l, 0, j)),
        out_shape=jax.ShapeDtypeStruct((depth, MOD_ROWS, n), F32),
        compiler_params=_cparams("parallel", "parallel"),
        name="modulation",
    )(cc, w_ada, b_ada.reshape(depth, 1, n))


def _prenorm_kernel(x_ref, g_ref, sh_ref, sc_ref, h_ref):
    x = x_ref[...]
    h = (x * _rms(x) * g_ref[...]) * (1.0 + sc_ref[0]) + sh_ref[0]
    h_ref[...] = h.astype(h_ref.dtype)


def _prenorm(x, g, sh, sc, geo):
    t, d = x.shape
    tm = geo["tm_row"]
    midx = geo["midx"](tm)
    return pl.pallas_call(
        _prenorm_kernel,
        grid=(t // tm,),
        in_specs=[pl.BlockSpec((tm, d), lambda i: (i, 0)),
                  pl.BlockSpec((1, d), lambda i: (0, 0)),
                  pl.BlockSpec((1, 1, d), lambda i: (midx(i), 0, 0)),
                  pl.BlockSpec((1, 1, d), lambda i: (midx(i), 0, 0))],
        out_specs=pl.BlockSpec((tm, d), lambda i: (i, 0)),
        out_shape=jax.ShapeDtypeStruct((t, d), BF16),
        compiler_params=_cparams("parallel"),
        name="prenorm",
    )(x, g.reshape(1, d), sh, sc)


def _proj_kernel(h_ref, w_ref, *rest, epi, group):
    acc = _dot(h_ref[...], w_ref[...])
    if epi == "none":
        (o_ref,) = rest
        o_ref[...] = acc.astype(o_ref.dtype)
    elif epi == "sigmoid":
        (o_ref,) = rest
        o_ref[...] = jax.nn.sigmoid(acc).astype(o_ref.dtype)
    else:
        cos_ref, sin_ref, cs_ref, o_ref = rest
        tn = acc.shape[1]
        reps = tn // LANES
        cos = jnp.concatenate([cos_ref[...]] * reps, axis=1)
        sin = jnp.concatenate([sin_ref[...]] * reps, axis=1)
        lane = lax.broadcasted_iota(jnp.int32, acc.shape, 1)
        first = (lane % (2 * group)) < group
        partner = jnp.where(first, pltpu.roll(acc, tn - group, 1), pltpu.roll(acc, group, 1))
        o_ref[...] = ((acc * cos + partner * sin) * cs_ref[...]).astype(o_ref.dtype)


def _project(h, w, out_dtype, geo, epi="none", rope=None, tn_cands=(1024, 512, 256, 128)):
    t, d = h.shape
    n = w.shape[1]
    tm = geo["tm_mm"]
    tn = _pick(tn_cands, n)
    in_specs = [pl.BlockSpec((tm, d), lambda i, j: (i, 0)),
                pl.BlockSpec((d, tn), lambda i, j: (0, j))]
    args = [h, w]
    group = 0
    if epi == "rope":
        cos, sin, colscale, group = rope
        tab = geo["tabidx"](tm)
        in_specs += [pl.BlockSpec((tm, LANES), lambda i, j: (tab(i), 0)),
                     pl.BlockSpec((tm, LANES), lambda i, j: (tab(i), 0)),
                     pl.BlockSpec((1, tn), lambda i, j: (0, j))]
        args += [cos, sin, colscale]
    return pl.pallas_call(
        functools.partial(_proj_kernel, epi=epi, group=group),
        grid=(t // tm, n // tn),
        in_specs=in_specs,
        out_specs=pl.BlockSpec((tm, tn), lambda i, j: (i, j)),
        out_shape=jax.ShapeDtypeStruct((t, n), out_dtype),
        compiler_params=_cparams("parallel", "arbitrary"),
        name=f"proj_{epi}",
    )(*args)


def _hgrn_constants(c):
    nlev = int(math.log2(c))
    assert 1 << nlev == c
    t = np.arange(c)[:, None]
    u = np.arange(c)[None, :]
    w = np.zeros((2, (2 + nlev) * c + ONES_ROWS, c), np.float32)
    lev = np.full((2, c, c), -1, np.int32)
    for d in range(2):
        if d == 0:
            cum, rem = u <= t, u > t
        else:
            cum, rem = u >= t, u < t
        w[d, 0:c] = cum
        w[d, c:2 * c] = rem
        for l in range(nlev):
            h = 1 << l
            base = (t // (2 * h)) * (2 * h)
            if d == 0:
                m = base + h - 1
                e = np.where(t > m, (u > m) & (u <= t), (u > t) & (u <= m))
                pair = (t // (2 * h) == u // (2 * h)) & (t % (2 * h) >= h) & (u % (2 * h) < h)
            else:
                m = base + h
                e = np.where(t < m, (u >= t) & (u < m), (u >= m) & (u < t))
                pair = (t // (2 * h) == u // (2 * h)) & (t % (2 * h) < h) & (u % (2 * h) >= h)
            w[d, (2 + l) * c:(3 + l) * c] = e
            lev[d][pair] = l
        w[d, (2 + nlev) * c:] = 1.0
        lev[d][np.arange(c), np.arange(c)] = nlev
    return jnp.asarray(w, BF16), jnp.asarray(lev), nlev


def _hgrn_kernel(zqig_ref, zf_ref, lb_ref, ng_ref, w_ref, lev_ref, o_ref,
                 st_ref, ofwd_ref, *, nlev, n_ctx_blk, n_lat_blk):
    c = lev_ref.shape[1]
    assert zqig_ref.shape[0] == c
    gw = zf_ref.shape[1]
    half = c // 2
    d = pl.program_id(2)
    j = pl.program_id(3)

    @pl.when(j == 0)
    def _():
        st_ref[...] = jnp.zeros_like(st_ref)

    slot = jnp.where(d == 0, j,
                     jnp.where(j < n_ctx_blk, n_ctx_blk - 1 - j,
                               2 * n_ctx_blk + n_lat_blk - 1 - j))
    srow = pl.ds(pl.multiple_of(slot * c, c), c)
    halves = (slice(0, half), slice(half, c))

    def run(direction):
        lb = lb_ref[0]
        lev = lev_ref[0]
        low_levels = list(range(nlev - 1))
        masks = [{l: lev[r, r] == l for l in low_levels + [nlev]} for r in halves]
        zf = zf_ref[...]
        sig = jax.nn.sigmoid(zf)
        f = lb + (1.0 - lb) * sig
        kk = (1.0 - lb) * (1.0 - sig)
        g = jnp.log(f)
        a = jnp.exp(_dot(w_ref[0], g.astype(BF16)))
        q = _silu(zqig_ref[:, 0:gw].astype(F32))
        v = zqig_ref[:, gw:2 * gw]
        q_top, k_top = (halves[1], halves[0]) if direction == 0 else (halves[0], halves[1])
        for hh in range(HGRN_HEADS_PER_STEP):
            ls = slice(hh * LANES, (hh + 1) * LANES)
            qh, kh, vh = q[:, ls], kk[:, ls], v[:, ls]
            diag = []
            for r, mk in zip(halves, masks):
                qr, kr = qh[r], kh[r]
                sc = jnp.where(mk[nlev], _dot_nt(qr.astype(BF16), kr.astype(BF16)), 0.0)
                for l in low_levels:
                    al = a[(2 + l) * c:(3 + l) * c, ls][r]
                    sc = jnp.where(mk[l], _dot_nt((qr * al).astype(BF16), (kr * al).astype(BF16)), sc)
                diag.append(sc.astype(BF16))
            at = a[(1 + nlev) * c:(2 + nlev) * c, ls]
            off = _dot_nt((qh[q_top] * at[q_top]).astype(BF16), (kh[k_top] * at[k_top]).astype(BF16))
            st = st_ref[hh]
            stb = st.astype(BF16)
            qs = (qh * a[0:c, ls]).astype(BF16)
            outs = [_dot(diag[i], vh[r]) + _dot_nt(qs[r], stb) for i, r in enumerate(halves)]
            i_top = 1 if direction == 0 else 0
            outs[i_top] = outs[i_top] + _dot(off.astype(BF16), vh[k_top])
            ut = _dot_tn(vh, (kh * a[c:2 * c, ls]).astype(BF16))
            dec = a[(2 + nlev) * c:(2 + nlev) * c + 1, ls]
            st_ref[hh] = st * dec + ut
            o = jnp.concatenate(outs, axis=0)
            if direction == 0:
                ofwd_ref[srow, ls] = o
            else:
                tot = ofwd_ref[srow, ls] + o
                gate = _silu(zqig_ref[:, 2 * gw + hh * LANES:2 * gw + (hh + 1) * LANES].astype(F32))
                o_ref[:, ls] = (tot * _rms(tot) * ng_ref[:, ls] * gate).astype(o_ref.dtype)

    @pl.when(d == 0)
    def _():
        run(0)

    @pl.when(d == 1)
    def _():
        run(1)


def _hgrn(z_nr, z_f, lb, norm_g, consts, geo):
    w_c, lev_c, nlev = consts
    bsz, s, l = geo["B"], geo["S"], geo["L"]
    t = z_nr.shape[0]
    r = geo["hgrn_rows"]
    nc, nl = l // r, s // r
    wcols = HGRN_HEADS_PER_STEP * LANES
    hp = BRANCH_W // wcols
    ctx0 = bsz * s // r

    def rowblk(b, d, j):
        slot = jnp.where(d == 0, j, jnp.where(j < nc, nc - 1 - j, 2 * nc + nl - 1 - j))
        return jnp.where(slot < nc, ctx0 + b * nc + slot, b * nl + slot - nc)

    def outmap(b, p, d, j):
        return (rowblk(b, 1, jnp.where(d == 0, 0, j)), p)

    kern = functools.partial(_hgrn_kernel, nlev=nlev, n_ctx_blk=nc, n_lat_blk=nl)
    return pl.pallas_call(
        kern,
        grid=(bsz, hp, 2, nc + nl),
        in_specs=[pl.BlockSpec((r, 3 * wcols), lambda b, p, d, j: (rowblk(b, d, j), p)),
                  pl.BlockSpec((r, wcols), lambda b, p, d, j: (rowblk(b, d, j), d * hp + p)),
                  pl.BlockSpec((1, 1, wcols), lambda b, p, d, j: (d, 0, p)),
                  pl.BlockSpec((1, wcols), lambda b, p, d, j: (0, p)),
                  pl.BlockSpec((1,) + w_c.shape[1:], lambda b, p, d, j: (d, 0, 0)),
                  pl.BlockSpec((1,) + lev_c.shape[1:], lambda b, p, d, j: (d, 0, 0))],
        out_specs=pl.BlockSpec((r, wcols), outmap),
        out_shape=jax.ShapeDtypeStruct((t, BRANCH_W), BF16),
        scratch_shapes=[pltpu.VMEM((HGRN_HEADS_PER_STEP, A_DV, A_DK), F32),
                        pltpu.VMEM((s + l, wcols), F32)],
        compiler_params=_cparams("parallel", "parallel", "arbitrary", "arbitrary"),
        name="hgrn2",
    )(z_nr, z_f, lb.reshape(2, 1, BRANCH_W),
      jnp.tile(norm_g, A_HEADS).reshape(1, BRANCH_W), w_c, lev_c)


def _diff_attn_kernel(lam_ref, q_ref, *rest, n_seg, out_scale):
    kv = rest[:2 * n_seg]
    g_ref, o_ref = rest[2 * n_seg], rest[2 * n_seg + 1]
    q = q_ref[...]
    lane = lax.broadcasted_iota(jnp.int32, q.shape, 1)
    zero = jnp.zeros_like(q)
    qs = (jnp.where(lane < B_DH, q, zero), jnp.where(lane >= B_DH, q, zero))
    es, inv = [], []
    for comp in range(2):
        s = [_dot_nt(qs[comp], kv[2 * i][...]) for i in range(n_seg)]
        m = s[0].max(axis=-1, keepdims=True)
        for x in s[1:]:
            m = jnp.maximum(m, x.max(axis=-1, keepdims=True))
        e = [jnp.exp2(x - m) for x in s]
        tot = e[0].sum(axis=-1, keepdims=True)
        for x in e[1:]:
            tot = tot + x.sum(axis=-1, keepdims=True)
        es.append(e)
        inv.append(1.0 / tot)
    r0 = inv[0]
    r1 = inv[1] * lam_ref[0, 0]
    o = None
    for i in range(n_seg):
        wgt = (es[0][i] * r0 - es[1][i] * r1).astype(BF16)
        part = _dot(wgt, kv[2 * i + 1][...])
        o = part if o is None else o + part
    o_ref[...] = (o * _rms(o) * (g_ref[...] * out_scale)).astype(o_ref.dtype)


def _diff_attn_pipe_kernel(lam_ref, q_ref, kl_ref, vl_ref, kc_ref, vc_ref, g_ref, o_ref,
                           s0_ref, s1_ref, m0_ref, m1_ref, e0_ref, e1_ref, wa_ref, wb_ref, ra_ref, rb_ref, *, tq, out_scale):
    s_len = kl_ref.shape[0]
    n = q_ref.shape[0] // tq
    lam = lam_ref[0, 0]
    gain = g_ref[...] * out_scale
    sa_ref, sb_ref = (s0_ref, m0_ref, e0_ref), (s1_ref, m1_ref, e1_ref)

    def rows(j):
        return pl.ds(pl.multiple_of(j * tq, tq), tq)

    def scores(j, sm_refs):
        s_ref, m_ref, _ = sm_refs
        q = q_ref[rows(j), :]
        lane = lax.broadcasted_iota(jnp.int32, q.shape, 1)
        zero = jnp.zeros_like(q)
        for comp, qc in enumerate((jnp.where(lane < B_DH, q, zero), jnp.where(lane >= B_DH, q, zero))):
            s_lat = _dot_nt(qc, kl_ref[...])
            s_ctx = _dot_nt(qc, kc_ref[...])
            s_ref[comp, :, 0:s_len] = s_lat
            s_ref[comp, :, s_len:] = s_ctx
            m_ref[comp] = jnp.maximum(s_lat.max(axis=-1, keepdims=True), s_ctx.max(axis=-1, keepdims=True))

    nk = s_len + kc_ref.shape[0]
    col_chunks = [(c0, min(c0 + SOFTMAX_COLS, nk)) for c0 in range(0, nk, SOFTMAX_COLS)]

    def softmax(sm_refs, w_ref, r_ref):
        s_ref, m_ref, e_ref = sm_refs
        ls = []
        for comp in range(2):
            m = m_ref[comp]
            acc = jnp.zeros((tq, LANES), F32)
            for c0, c1 in col_chunks:
                e = jnp.exp2(s_ref[comp, :, c0:c1] - m)
                for k in range(0, c1 - c0, LANES):
                    acc = acc + e[:, k:k + LANES]
                e_ref[comp, :, c0:c1] = e.astype(BF16)
            ls.append(acc.sum(axis=-1, keepdims=True))
        coef = (lam * ls[0] / ls[1]).astype(BF16)
        for c0, c1 in col_chunks:
            w_ref[:, c0:c1] = e_ref[0, :, c0:c1] - coef * e_ref[1, :, c0:c1]
        r_ref[...] = 1.0 / ls[0]

    def values(j, w_ref, r_ref):
        o = (_dot(w_ref[:, 0:s_len], vl_ref[...]) + _dot(w_ref[:, s_len:], vc_ref[...])) * r_ref[...]
        o_ref[rows(j), :] = (o * _rms(o) * gain).astype(o_ref.dtype)

    scores(0, sa_ref)
    scores(1, sb_ref)
    softmax(sa_ref, wa_ref, ra_ref)

    def pair(u, carry):
        t = 2 * u
        values(t - 2, wa_ref, ra_ref)
        softmax(sb_ref, wb_ref, rb_ref)
        scores(t, sa_ref)
        values(t - 1, wb_ref, rb_ref)
        softmax(sa_ref, wa_ref, ra_ref)
        scores(t + 1, sb_ref)
        return carry

    lax.fori_loop(1, n // 2, pair, 0)
    softmax(sb_ref, wb_ref, rb_ref)
    values(n - 2, wa_ref, ra_ref)
    values(n - 1, wb_ref, rb_ref)


def _diff_attn_lat(z_rb, z_nr, lam, subln_g, lambda_init, geo):
    bsz, s, l = geo["B"], geo["S"], geo["L"]
    t = z_rb.shape[0]
    ctx0 = bsz * s // l
    kcol, vcol = B_HEADS, 3 * B_HEADS
    tq = geo["tq_b"]
    assert (s // tq) % 2 == 0
    nk = s + l
    return pl.pallas_call(
        functools.partial(_diff_attn_pipe_kernel, tq=tq, out_scale=1.0 - lambda_init),
        grid=(bsz, B_HEADS),
        in_specs=[pl.BlockSpec(memory_space=pltpu.SMEM),
                  pl.BlockSpec((s, LANES), lambda b, h: (b, h)),
                  pl.BlockSpec((s, LANES), lambda b, h: (b, kcol + h)),
                  pl.BlockSpec((s, LANES), lambda b, h: (b, vcol + h)),
                  pl.BlockSpec((l, LANES), lambda b, h: (ctx0 + b, kcol + h)),
                  pl.BlockSpec((l, LANES), lambda b, h: (ctx0 + b, vcol + h)),
                  pl.BlockSpec((1, LANES), lambda b, h: (0, 0))],
        out_specs=pl.BlockSpec((s, LANES), lambda b, h: (b, h)),
        out_shape=jax.ShapeDtypeStruct((t, BRANCH_W), BF16),
        scratch_shapes=[pltpu.VMEM((2, tq, nk), F32), pltpu.VMEM((2, tq, nk), F32),
                        pltpu.VMEM((2, tq, 1), F32), pltpu.VMEM((2, tq, 1), F32),
                        pltpu.VMEM((2, tq, nk), BF16), pltpu.VMEM((2, tq, nk), BF16),
                        pltpu.VMEM((tq, nk), BF16), pltpu.VMEM((tq, nk), BF16),
                        pltpu.VMEM((tq, 1), F32), pltpu.VMEM((tq, 1), F32)],
        compiler_params=_cparams("parallel", "parallel"),
        name="diff_attn_lat",
    )(lam.reshape(1, 1), z_rb, z_rb, z_nr, z_rb, z_nr, subln_g.reshape(1, 2 * B_DH))


def _diff_attn(z_rb, z_nr, lam, subln_g, lambda_init, geo, prev=None):
    bsz, s, l = geo["B"], geo["S"], geo["L"]
    t = z_rb.shape[0]
    ctx0 = bsz * s // l
    kcol, vcol = B_HEADS, 3 * B_HEADS
    g2 = subln_g.reshape(1, 2 * B_DH)
    lam2 = lam.reshape(1, 1)
    if prev is None:
        return _diff_attn_lat(z_rb, z_nr, lam, subln_g, lambda_init, geo)

    def kern(lam_ref, q_ref, k_ref, v_ref, g_ref, prev_ref, o_ref):
        del prev_ref
        _diff_attn_kernel(lam_ref, q_ref, k_ref, v_ref, g_ref, o_ref, n_seg=1, out_scale=1.0 - lambda_init)

    return pl.pallas_call(
        kern,
        grid=(bsz, B_HEADS),
        in_specs=[pl.BlockSpec(memory_space=pltpu.SMEM),
                  pl.BlockSpec((l, LANES), lambda b, h: (ctx0 + b, h)),
                  pl.BlockSpec((l, LANES), lambda b, h: (ctx0 + b, kcol + h)),
                  pl.BlockSpec((l, LANES), lambda b, h: (ctx0 + b, vcol + h)),
                  pl.BlockSpec((1, LANES), lambda b, h: (0, 0)),
                  pl.BlockSpec(memory_space=pl.ANY)],
        out_specs=pl.BlockSpec((l, LANES), lambda b, h: (ctx0 + b, h)),
        out_shape=jax.ShapeDtypeStruct((t, BRANCH_W), BF16),
        input_output_aliases={5: 0},
        compiler_params=_cparams("parallel", "parallel"),
        name="diff_attn_ctx",
    )(lam2, z_rb, z_rb, z_nr, g2, prev)


def _stack_heads(q_ref, r0, rows):
    return jnp.concatenate([q_ref[r0:r0 + rows, r * LANES:(r + 1) * LANES] for r in range(C_GROUP)], axis=0)


def _sink_column(sink_ref, g, rows_per_head):
    rid = lax.broadcasted_iota(jnp.int32, (C_GROUP * rows_per_head, 1), 0) // rows_per_head
    col = jnp.full(rid.shape, sink_ref[g * C_GROUP] * LOG2E, F32)
    for r in range(1, C_GROUP):
        col = jnp.where(rid == r, sink_ref[g * C_GROUP + r] * LOG2E, col)
    return col


def _lane_groups(x):
    return [x[:, k:k + LANES] for k in range(0, x.shape[1], LANES)]


def _sink_attend(scores, values, sink_col, o_ref, r0, rows_per_head):
    groups = [grp for x in scores for grp in _lane_groups(x)]
    mvec = groups[0]
    for grp in groups[1:]:
        mvec = jnp.maximum(mvec, grp)
    m = jnp.maximum(sink_col, mvec.max(axis=-1, keepdims=True))
    e = [jnp.exp2(x - m) for x in scores]
    egroups = [grp for x in e for grp in _lane_groups(x)]
    svec = egroups[0]
    for grp in egroups[1:]:
        svec = svec + grp
    inv = 1.0 / (jnp.exp2(sink_col - m) + svec.sum(axis=-1, keepdims=True))
    o = None
    for x, v in zip(e, values):
        part = _dot(x.astype(BF16), v)
        o = part if o is None else o + part
    o = o * inv
    for r in range(C_GROUP):
        o_ref[r0:r0 + rows_per_head, r * LANES:(r + 1) * LANES] = (
            o[r * rows_per_head:(r + 1) * rows_per_head].astype(o_ref.dtype))


def _win_attn_kernel(sink_ref, q_ref, kc_ref, vc_ref, kp_ref, km_ref, kn_ref, vp_ref, vm_ref, vn_ref,
                     o_ref, *, n_step):
    g = pl.program_id(1)
    n = pl.program_id(2)
    n_sub = q_ref.shape[0] // C_BLOCK
    rows = C_GROUP * C_BLOCK
    tq = lax.broadcasted_iota(jnp.int32, (rows, C_BLOCK), 0) % C_BLOCK
    kj = lax.broadcasted_iota(jnp.int32, (rows, C_BLOCK), 1)
    sink_col = _sink_column(sink_ref, g, C_BLOCK)

    def blk(ref, a):
        return ref[a * C_BLOCK:(a + 1) * C_BLOCK, :]

    kband = [kp_ref[...]] + [blk(km_ref, a) for a in range(n_sub)] + [kn_ref[...]]
    vband = [vp_ref[...]] + [blk(vm_ref, a) for a in range(n_sub)] + [vn_ref[...]]
    for a in range(n_sub):
        q4 = _stack_heads(q_ref, a * C_BLOCK, C_BLOCK)
        has_prev = True if a > 0 else n > 0
        has_next = True if a < n_sub - 1 else n < n_step - 1
        sp = jnp.where(jnp.logical_and(kj >= tq, has_prev), _dot_nt(q4, kband[a]), NEG_BIG)
        sn = jnp.where(jnp.logical_and(kj <= tq, has_next), _dot_nt(q4, kband[a + 2]), NEG_BIG)
        scores = [_dot_nt(q4, kc_ref[...]), sp, _dot_nt(q4, kband[a + 1]), sn]
        values = [vc_ref[...], vband[a], vband[a + 1], vband[a + 2]]
        _sink_attend(scores, values, sink_col, o_ref, a * C_BLOCK, C_BLOCK)


def _win_attn_ctx_kernel(sink_ref, q_ref, kc_ref, vc_ref, prev_ref, o_ref):
    del prev_ref
    g = pl.program_id(1)
    rows_per_head = q_ref.shape[0]
    q4 = _stack_heads(q_ref, 0, rows_per_head)
    _sink_attend([_dot_nt(q4, kc_ref[...])], [vc_ref[...]],
                 _sink_column(sink_ref, g, rows_per_head), o_ref, 0, rows_per_head)


def _win_attn(z_rc, z_nr, sink, geo, prev=None):
    bsz, s, l = geo["B"], geo["S"], geo["L"]
    t = z_rc.shape[0]
    n_blk = s // C_BLOCK
    qw = C_GROUP * LANES
    kcol = C_Q_HEADS
    vcol = (4 * BRANCH_W) // LANES
    ctx0 = bsz * s // l
    smem = pl.BlockSpec(memory_space=pltpu.SMEM)
    if prev is None:
        qb = geo["win_rows"]
        sub = qb // C_BLOCK
        n_step = s // qb

        def edge(colblk0, shift):
            def imap(b, g, n):
                return (b * n_blk + jnp.clip(n * sub + shift, 0, n_blk - 1), colblk0 + g)
            return pl.BlockSpec((C_BLOCK, LANES), imap)

        def mid(colblk0):
            return pl.BlockSpec((qb, LANES), lambda b, g, n: (b * n_step + n, colblk0 + g))

        in_specs = [smem,
                    pl.BlockSpec((qb, qw), lambda b, g, n: (b * n_step + n, g)),
                    pl.BlockSpec((l, LANES), lambda b, g, n: (ctx0 + b, kcol + g)),
                    pl.BlockSpec((l, LANES), lambda b, g, n: (ctx0 + b, vcol + g)),
                    edge(kcol, -1), mid(kcol), edge(kcol, sub),
                    edge(vcol, -1), mid(vcol), edge(vcol, sub)]
        args = [sink, z_rc, z_rc, z_nr, z_rc, z_rc, z_rc, z_nr, z_nr, z_nr]
        return pl.pallas_call(
            functools.partial(_win_attn_kernel, n_step=n_step),
            grid=(bsz, C_KV_HEADS, n_step),
            in_specs=in_specs,
            out_specs=pl.BlockSpec((qb, qw), lambda b, g, n: (b * n_step + n, g)),
            out_shape=jax.ShapeDtypeStruct((t, BRANCH_W), BF16),
            compiler_params=_cparams("parallel", "parallel", "arbitrary"),
            name="win_attn_lat",
        )(*args)
    in_specs = [smem,
                pl.BlockSpec((l, qw), lambda b, g: (ctx0 + b, g)),
                pl.BlockSpec((l, LANES), lambda b, g: (ctx0 + b, kcol + g)),
                pl.BlockSpec((l, LANES), lambda b, g: (ctx0 + b, vcol + g)),
                pl.BlockSpec(memory_space=pl.ANY)]
    return pl.pallas_call(
        _win_attn_ctx_kernel,
        grid=(bsz, C_KV_HEADS),
        in_specs=in_specs,
        out_specs=pl.BlockSpec((l, qw), lambda b, g: (ctx0 + b, g)),
        out_shape=jax.ShapeDtypeStruct((t, BRANCH_W), BF16),
        input_output_aliases={4: 0},
        compiler_params=_cparams("parallel", "parallel"),
        name="win_attn_ctx",
    )(sink, z_rc, z_rc, z_nr, prev)


def _merge_kernel(a_ref, b_ref, c_ref, w_ref, ga_ref, gb_ref, gc_ref, o_ref):
    acc = ga_ref[...].astype(F32) * _dot(a_ref[...], w_ref[0])
    acc = acc + gb_ref[...].astype(F32) * _dot(b_ref[...], w_ref[1])
    acc = acc + gc_ref[...].astype(F32) * _dot(c_ref[...], w_ref[2])
    o_ref[...] = acc.astype(o_ref.dtype)


def _merge(oa, ob, oc, gates, w_branch, geo, rows):
    t = rows
    tm = geo["tm_mm"]
    tn = 512
    nj = D_MODEL // tn
    br = pl.BlockSpec((tm, BRANCH_W), lambda i, j: (i, 0))

    def gate(n):
        return pl.BlockSpec((tm, tn), lambda i, j: (i, n * nj + j))

    return pl.pallas_call(
        _merge_kernel,
        grid=(t // tm, nj),
        in_specs=[br, br, br,
                  pl.BlockSpec((N_BRANCH, BRANCH_W, tn), lambda i, j: (0, 0, j)),
                  gate(0), gate(1), gate(2)],
        out_specs=pl.BlockSpec((tm, tn), lambda i, j: (i, j)),
        out_shape=jax.ShapeDtypeStruct((t, D_MODEL), BF16),
        compiler_params=_cparams("parallel", "arbitrary"),
        name="branch_merge",
    )(oa, ob, oc, w_branch, gates, gates, gates)


def _residual_epilogue(y, x_ref, gpost_ref, gate_ref, gpre_ref, sh_ref, sc_ref, xo_ref, ho_ref):
    x1 = x_ref[...] + gate_ref[0] * (y * _rms(y) * gpost_ref[...])
    xo_ref[...] = x1
    h = (x1 * _rms(x1) * gpre_ref[...]) * (1.0 + sc_ref[0]) + sh_ref[0]
    ho_ref[...] = h.astype(ho_ref.dtype)


def _outproj_kernel(m_ref, w_ref, x_ref, gpost_ref, gate_ref, gpre_ref, sh_ref, sc_ref, xo_ref, ho_ref):
    tm = m_ref.shape[0]
    n_slice = 4
    rs = tm // n_slice
    for k in range(n_slice):
        r = slice(k * rs, (k + 1) * rs)
        y = _dot(m_ref[r, :], w_ref[...])
        _residual_epilogue(y, x_ref.at[r, :], gpost_ref, gate_ref, gpre_ref, sh_ref, sc_ref,
                           xo_ref.at[r, :], ho_ref.at[r, :])


def _down_kernel(a_ref, w_ref, x_ref, gpost_ref, gate_ref, gpre_ref, sh_ref, sc_ref, xo_ref, ho_ref, y_ref):
    j = pl.program_id(1)
    nj = y_ref.shape[0]
    y_ref[j] = _dot(a_ref[...], w_ref[...])

    @pl.when(j == nj - 1)
    def _():
        y = jnp.concatenate([y_ref[jj] for jj in range(nj)], axis=1)
        _residual_epilogue(y, x_ref, gpost_ref, gate_ref, gpre_ref, sh_ref, sc_ref, xo_ref, ho_ref)


def _residual_specs(tm, d, midx, grid_rank):
    if grid_rank == 1:
        row = lambda i: (i, 0)
        const = lambda i: (0, 0)
        mod = lambda i: (midx(i), 0, 0)
    else:
        row = lambda i, k: (i, 0)
        const = lambda i, k: (0, 0)
        mod = lambda i, k: (midx(i), 0, 0)
    ins = [pl.BlockSpec((tm, d), row), pl.BlockSpec((1, d), const), pl.BlockSpec((1, 1, d), mod),
           pl.BlockSpec((1, d), const), pl.BlockSpec((1, 1, d), mod), pl.BlockSpec((1, 1, d), mod)]
    outs = [pl.BlockSpec((tm, d), row), pl.BlockSpec((tm, d), row)]
    return ins, outs


def _outproj(m, w, x, gpost, gate, gpre, sh, sc, geo, rows):
    t, d = rows, x.shape[1]
    tm = geo["tm_row"]
    r_ins, r_outs = _residual_specs(tm, d, geo["midx"](tm), 1)
    return pl.pallas_call(
        _outproj_kernel,
        grid=(t // tm,),
        in_specs=[pl.BlockSpec((tm, d), lambda i: (i, 0)), pl.BlockSpec((d, d), lambda i: (0, 0))] + r_ins,
        out_specs=r_outs,
        out_shape=[jax.ShapeDtypeStruct((t, d), F32), jax.ShapeDtypeStruct((t, d), BF16)],
        compiler_params=_cparams("parallel"),
        name="outproj_residual",
    )(m, w, x, gpost.reshape(1, d), gate, gpre.reshape(1, d), sh, sc)


def _ffn_down(act, w, x, gpost, gate, gpre, sh, sc, geo, rows):
    t, d = rows, x.shape[1]
    kf = act.shape[1]
    tm = geo["tm_row"]
    tn = _pick((512, 256, 128), d)
    r_ins, r_outs = _residual_specs(tm, d, geo["midx"](tm), 2)
    return pl.pallas_call(
        _down_kernel,
        grid=(t // tm, d // tn),
        in_specs=[pl.BlockSpec((tm, kf), lambda i, j: (i, 0)), pl.BlockSpec((kf, tn), lambda i, j: (0, j))] + r_ins,
        out_specs=r_outs,
        out_shape=[jax.ShapeDtypeStruct((t, d), F32), jax.ShapeDtypeStruct((t, d), BF16)],
        scratch_shapes=[pltpu.VMEM((d // tn, tm, tn), F32)],
        compiler_params=_cparams("parallel", "arbitrary"),
        name="ffn_down_residual",
    )(act, w, x, gpost.reshape(1, d), gate, gpre.reshape(1, d), sh, sc)


HALO = 16


def _ffn_up_kernel(hp_ref, h_ref, hn_ref, wa_ref, wv_ref, cwa_ref, cwv_ref, cba_ref, cbv_ref, o_ref,
                   *, n_lat_blk, s, l):
    i = pl.program_id(0)
    tm = h_ref.shape[0]
    hext = jnp.concatenate([hp_ref[...], h_ref[...], hn_ref[...]], axis=0)
    row = lax.broadcasted_iota(jnp.int32, (tm, 1), 0)

    def edges(blk, seq):
        if seq >= tm:
            per = seq // tm
            return tm - 1, blk % per == 0, blk % per == per - 1
        assert seq & (seq - 1) == 0, "short sequences must have power-of-two length"
        return seq - 1, True, True

    is_lat = i < n_lat_blk
    pm_l, bf_l, bl_l = edges(i, s)
    pm_c, bf_c, bl_c = edges(i - n_lat_blk, l)
    pm = jnp.where(is_lat, pm_l, pm_c)
    is_first = jnp.logical_and((row & pm) == 0, jnp.where(is_lat, bf_l, bf_c))
    is_last = jnp.logical_and((row & pm) == pm, jnp.where(is_lat, bl_l, bl_c))

    def conv(w_ref, cw_ref, cb_ref):
        u = _dot(hext, w_ref[...])
        prev = jnp.where(is_first, 0.0, u[HALO - 1:HALO - 1 + tm])
        nxt = jnp.where(is_last, 0.0, u[HALO + 1:HALO + 1 + tm])
        cw = cw_ref[...]
        return prev * cw[0:1] + u[HALO:HALO + tm] * cw[1:2] + nxt * cw[2:3] + cb_ref[...]

    a = conv(wa_ref, cwa_ref, cba_ref)
    v = conv(wv_ref, cwv_ref, cbv_ref)
    o_ref[...] = (_silu(a) * v).astype(o_ref.dtype)


def _ffn_up(h, w_up, conv_w, conv_b, geo, rows):
    t, d = rows, h.shape[1]
    tm = geo["tm_mm"]
    tn = _pick((512, 256, 128), D_FF)
    nj = D_FF // tn
    n_halo = t // HALO
    per = tm // HALO
    cb = conv_b.reshape(1, 2 * D_FF)
    kern = functools.partial(_ffn_up_kernel, n_lat_blk=geo["B"] * geo["S"] // tm, s=geo["S"], l=geo["L"])
    return pl.pallas_call(
        kern,
        grid=(t // tm, nj),
        in_specs=[pl.BlockSpec((HALO, d), lambda i, j: (jnp.maximum(i * per - 1, 0), 0)),
                  pl.BlockSpec((tm, d), lambda i, j: (i, 0)),
                  pl.BlockSpec((HALO, d), lambda i, j: (jnp.minimum((i + 1) * per, n_halo - 1), 0)),
                  pl.BlockSpec((d, tn), lambda i, j: (0, j)),
                  pl.BlockSpec((d, tn), lambda i, j: (0, nj + j)),
                  pl.BlockSpec((3, tn), lambda i, j: (0, j)),
                  pl.BlockSpec((3, tn), lambda i, j: (0, nj + j)),
                  pl.BlockSpec((1, tn), lambda i, j: (0, j)),
                  pl.BlockSpec((1, tn), lambda i, j: (0, nj + j))],
        out_specs=pl.BlockSpec((tm, tn), lambda i, j: (i, j)),
        out_shape=jax.ShapeDtypeStruct((t, D_FF), BF16),
        compiler_params=_cparams("parallel", "arbitrary"),
        name="ffn_up_conv",
    )(h, h, h, w_up, w_up, conv_w, conv_w, cb, cb)


def _rope_tables(s, head_dim, tm):
    quarter = head_dim // 4
    pos = np.arange(s)
    inv = np.power(np.float32(ROPE_BASE), -np.arange(quarter, dtype=np.float32) / quarter).astype(np.float32)
    row = (pos // GRID_W).astype(np.float32)[:, None] * inv
    col = (pos % GRID_W).astype(np.float32)[:, None] * inv
    ang = jnp.asarray(np.concatenate([row, row, col, col], axis=1))
    sign = np.concatenate([-np.ones(quarter), np.ones(quarter)] * 2).astype(np.float32)
    cos = jnp.cos(ang)
    sin = jnp.sin(ang) * sign
    reps = LANES // head_dim
    cos = jnp.tile(cos, (1, reps))
    sin = jnp.tile(sin, (1, reps))
    cos = jnp.concatenate([cos, jnp.ones((tm, LANES), F32)], axis=0)
    sin = jnp.concatenate([sin, jnp.zeros((tm, LANES), F32)], axis=0)
    return cos, sin, quarter


def _geometry(bsz, s, l):
    nl, nc = bsz * s, bsz * l

    def midx(tm):
        nlat, per = nl // tm, s // tm
        return lambda i: jnp.where(i < nlat, 1 + i // per, 0)

    def tabidx(tm):
        nlat, per = nl // tm, s // tm
        return lambda i: jnp.where(i < nlat, i % per, per)

    return {"B": bsz, "S": s, "L": l,
            "tm_mm": _pick((1024, 512, 256, 128), s, nc),
            "tm_row": _pick((512, 256, 128), s, nc),
            "tq_b": _pick((256, 128, 64), s // 2),
            "hgrn_rows": _pick((256, 128, 64), s, l),
            "win_rows": _pick((512, 256, 128), s),
            "midx": midx, "tabidx": tabidx}


def kernel(x, c, ctx, c_ctx, w_ada, b_ada, g_pre_mix, g_post_mix, g_pre_ffn, g_post_ffn, w_in, a_lb_logits, a_norm_g, b_lambda, b_subln_g, c_sink, w_branch, w_out, w_up, conv_w, conv_b, w_down):
    bsz, s, d = x.shape
    l = ctx.shape[1]
    depth = w_in.shape[0]
    assert d == D_MODEL and s % GRID_W == 0 and s % C_BLOCK == 0 and bsz + 1 <= MOD_ROWS
    geo = _geometry(bsz, s, l)
    nl = bsz * s

    xa = jnp.concatenate([x.reshape(nl, d), ctx.reshape(bsz * l, d)], axis=0)

    cc = jnp.concatenate([c_ctx[None], c, jnp.zeros((MOD_ROWS - 1 - bsz, d), F32)], axis=0)
    mod = _modulation(cc, w_ada, b_ada).reshape(depth, MOD_ROWS, 6, 1, d)

    def modv(layer, k):
        return mod[layer, :, k]

    p = jax.nn.softmax(a_lb_logits.astype(F32), axis=1)
    lower = jnp.cumsum(p, axis=1) - p[:, :1]
    hconsts = _hgrn_constants(geo["hgrn_rows"])

    tm = geo["tm_mm"]
    cos_b, sin_b, grp_b = _rope_tables(s, B_DH, tm)
    cos_c, sin_c, grp_c = _rope_tables(s, C_DH, tm)
    scale_b = jnp.concatenate([jnp.full((BRANCH_W,), B_DH ** -0.5 * LOG2E, F32),
                               jnp.ones((BRANCH_W,), F32)]).reshape(1, -1)
    scale_c = jnp.concatenate([jnp.full((BRANCH_W,), C_DH ** -0.5 * LOG2E, F32),
                               jnp.ones((C_KV_HEADS * C_DH,), F32)]).reshape(1, -1)

    e = np.cumsum((0, 1024, 1024, 1024, 1024, 1024, 1024, 1024, 1024, 1024, 256, 256, 6144))
    h = _prenorm(xa, g_pre_mix[0], modv(0, 0), modv(0, 1), geo)
    for layer in range(depth):
        wi = w_in[layer]
        gw = HGRN_HEADS_PER_STEP * LANES
        a_cols = [wi[:, e[k] + p * gw:e[k] + (p + 1) * gw] for p in range(BRANCH_W // gw) for k in (0, 3, 4)]
        w_nr = jnp.concatenate(a_cols + [wi[:, e[7]:e[8]], wi[:, e[10]:e[11]]], axis=1).astype(BF16)
        w_f = wi[:, e[1]:e[3]].astype(BF16)
        w_rb = wi[:, e[5]:e[7]].astype(BF16)
        w_rc = wi[:, e[8]:e[10]].astype(BF16)
        w_g = wi[:, e[11]:e[12]].astype(BF16)
        lambda_init = 0.8 - 0.6 * math.exp(-0.3 * layer)
        lv = b_lambda[layer].astype(F32)
        lam = jnp.exp(jnp.sum(lv[0] * lv[1])) - jnp.exp(jnp.sum(lv[2] * lv[3])) + lambda_init

        z_nr = _project(h, w_nr, BF16, geo, tn_cands=(2176, 256, 128))
        z_f = _project(h, w_f, F32, geo)
        z_rb = _project(h, w_rb, BF16, geo, epi="rope", rope=(cos_b, sin_b, scale_b, grp_b))
        z_rc = _project(h, w_rc, BF16, geo, epi="rope", rope=(cos_c, sin_c, scale_c, grp_c),
                        tn_cands=(1280, 256, 128))
        gates = _project(h, w_g, BF16, geo, epi="sigmoid")

        oa = _hgrn(z_nr, z_f, lower[:, layer], a_norm_g[layer], hconsts, geo)
        ob = _diff_attn(z_rb, z_nr, lam, b_subln_g[layer], lambda_init, geo)
        oc = _win_attn(z_rc, z_nr, c_sink[layer].astype(F32), geo)
        need_ctx = layer < depth - 1
        rows = xa.shape[0] if need_ctx else nl
        if need_ctx:
            ob = _diff_attn(z_rb, z_nr, lam, b_subln_g[layer], lambda_init, geo, prev=ob)
            oc = _win_attn(z_rc, z_nr, c_sink[layer].astype(F32), geo, prev=oc)

        m = _merge(oa, ob, oc, gates, w_branch[layer].astype(BF16), geo, rows)
        xa, h = _outproj(m, w_out[layer].astype(BF16), xa, g_post_mix[layer], modv(layer, 2),
                         g_pre_ffn[layer], modv(layer, 3), modv(layer, 4), geo, rows)
        act = _ffn_up(h, w_up[layer].astype(BF16), conv_w[layer], conv_b[layer], geo, rows)
        nxt = min(layer + 1, depth - 1)
        xa, h = _ffn_down(act, w_down[layer].astype(BF16), xa, g_post_ffn[layer], modv(layer, 5),
                          g_pre_mix[nxt], modv(nxt, 0), modv(nxt, 1), geo, rows)
    return xa.reshape(bsz, s, d)
```
